```python
import math
import jax, jax.numpy as jnp
from jax import lax
import numpy as np

D_MODEL = 1024
BATCH = 4
SEQ = 4096
DEPTH = 4
DEC_BATCH = 128
DEC_SEQ = 8
PAST_LEN = 8192
PAGE_SIZE = 128

N_HEADS = 16
HEAD_DIM = D_MODEL // N_HEADS
N_MIXERS = 3
N_LAYERS_A = (DEPTH + 2) // 3
N_LAYERS_B = (DEPTH + 1) // 3
N_LAYERS_C = DEPTH // 3
Q_BLOCK = 128

NSA_KV_HEADS = 2
NSA_CMP_BLOCK = 64
NSA_TOPK = 16
NSA_WINDOW = 512
NSA_FORCED_SCORE = float(N_HEADS + 1)

SB_KV_HEADS = 4

SWA_KV_HEADS = 2
SWA_WINDOW = 128

REL_BUCKETS = 32
REL_MAX_DIST = 128

MOE_GROUPS = 4
MOE_EXPERTS_PER_GROUP = 8
MOE_EXPERTS = MOE_GROUPS * MOE_EXPERTS_PER_GROUP
MOE_TOPK = 2
MOE_D_FF = 512
MOE_BLOCK = 128

DEEPNORM_ALPHA = (2 * DEPTH) ** 0.25
DEEPNORM_BETA = (8 * DEPTH) ** -0.25
LN_EPS = 1e-5
NEG_INF = -1e30

NSA_IN_SIZES = (N_HEADS * HEAD_DIM,) + (NSA_KV_HEADS * HEAD_DIM,) * 6 + (3 * N_HEADS,)
SB_IN_SIZES = (N_HEADS * HEAD_DIM, SB_KV_HEADS * HEAD_DIM, SB_KV_HEADS * HEAD_DIM)
SWA_IN_SIZES = (N_HEADS * HEAD_DIM, SWA_KV_HEADS * HEAD_DIM, SWA_KV_HEADS * HEAD_DIM)

kernel_name = 'hybrid_nsa_stickbreak_swa_hmoe_step'


def split_cols(a, sizes):
    return jnp.split(a, np.cumsum(sizes)[:-1].tolist(), axis=-1)


def layer_norm(x, g, b):
    xf = x.astype(jnp.float32)
    mu = jnp.mean(xf, -1, keepdims=True)
    var = jnp.mean(jnp.square(xf - mu), -1, keepdims=True)
    y = (xf - mu) * lax.rsqrt(var + LN_EPS) * g.astype(jnp.float32) + b.astype(jnp.float32)
    return y.astype(x.dtype)


def rel_bucket(dist):
    n = jnp.maximum(dist, 0)
    exact = REL_BUCKETS // 2
    logv = jnp.log(jnp.maximum(n, 1).astype(jnp.float32) / exact) / math.log(REL_MAX_DIST / exact)
    large = jnp.minimum(exact + (logv * (REL_BUCKETS - exact)).astype(jnp.int32), REL_BUCKETS - 1)
    return jnp.where(n < exact, n, large)


def rel_bias_shared(rel_bias, dist, n_groups):
    b = rel_bias[rel_bucket(dist)].astype(jnp.float32)
    b = b.reshape(dist.shape + (n_groups, N_HEADS // n_groups))
    return jnp.transpose(b, (0, 2, 3, 1))


def masked_softmax(s, valid):
    p = jax.nn.softmax(jnp.where(valid, s, NEG_INF), axis=-1)
    return jnp.where(valid, p, 0.0)


def to_blocks(x, qb):
    n, s = x.shape[:2]
    return jnp.moveaxis(x.reshape((n, s // qb, qb) + x.shape[2:]), 1, 0)


def from_blocks(xb):
    nq, n, qb = xb.shape[:3]
    return jnp.moveaxis(xb, 0, 1).reshape((n, nq * qb) + xb.shape[3:])


def band_blocks(k, window):
    n, s = k.shape[:2]
    nq, r = s // Q_BLOCK, window // Q_BLOCK
    kp = jnp.pad(k, ((0, 0), (window, 0)) + ((0, 0),) * (k.ndim - 2))
    kb = kp.reshape((n, nq + r, Q_BLOCK) + k.shape[2:])
    kb = jnp.concatenate([kb[:, i:i + nq] for i in range(r + 1)], axis=2)
    return jnp.moveaxis(kb, 1, 0)


def gather_pages(pool, slot, page_table):
    rows = pool[slot, page_table]
    return rows.reshape((page_table.shape[0], -1) + pool.shape[3:])


def split_heads(q, n_groups):
    return q.reshape(q.shape[:2] + (n_groups, N_HEADS // n_groups, HEAD_DIM))


def ada_modulation(c, w, b):
    m = (jax.nn.silu(c) @ w + b)[:, None, :]
    return jnp.split(m, 6, axis=-1)


def compress_blocks(rows, w_phi):
    n, l, g, d = rows.shape
    rb = rows.reshape(n, l // NSA_CMP_BLOCK, NSA_CMP_BLOCK, g, d)
    return jnp.einsum('nbjgd,gjde->nbge', rb, w_phi)


def nsa_core(q, q_pos, kc, vc, c_end, ks_t, vs_t, kw, vw, kw_pos, gate, rel_bias):
    n, nq, g, hg, hd = q.shape
    nb = kc.shape[1]
    scale = hd ** -0.5
    t = q_pos[:, None]
    dist_c = t - c_end[None, :]
    valid_c = (dist_c >= 0)[None, :, None, None, :]
    s_c = jnp.einsum('nqghd,nbgd->nqghb', q, kc).astype(jnp.float32) * scale + rel_bias_shared(rel_bias, dist_c, g)
    p_c = masked_softmax(s_c, valid_c)
    o_c = jnp.einsum('nqghb,nbgd->nqghd', p_c.astype(vc.dtype), vc)
    blk = jnp.arange(nb)[None, :]
    cur = (q_pos // NSA_CMP_BLOCK)[:, None]
    forced = ((blk == cur) | (blk == cur - 1) | (blk == 0))[None, :, None, :]
    cand = (blk <= cur)[None, :, None, :]
    score = jnp.where(cand, jnp.where(forced, NSA_FORCED_SCORE, p_c.sum(axis=3)), -jnp.inf)
    top_val, top_idx = lax.top_k(score, min(NSA_TOPK, nb))
    ni = jnp.arange(n)[:, None, None, None]
    gi = jnp.arange(g)[None, None, :, None]
    n_sel = top_idx.shape[-1] * NSA_CMP_BLOCK
    k_sel = ks_t[ni, gi, top_idx].reshape(n, nq, g, n_sel, hd)
    v_sel = vs_t[ni, gi, top_idx].reshape(n, nq, g, n_sel, hd)
    pos = (top_idx[..., None] * NSA_CMP_BLOCK + jnp.arange(NSA_CMP_BLOCK)).reshape(n, nq, g, n_sel)
    dist_s = q_pos[None, :, None, None] - pos
    valid_s = jnp.repeat(jnp.isfinite(top_val), NSA_CMP_BLOCK, axis=-1) & (dist_s >= 0)
    bt = jnp.transpose(rel_bias.reshape(REL_BUCKETS, g, hg), (1, 0, 2))
    b_s = jnp.swapaxes(bt[gi, rel_bucket(dist_s)], -1, -2).astype(jnp.float32)
    s_s = jnp.einsum('nqghd,nqgkd->nqghk', q, k_sel).astype(jnp.float32) * scale + b_s
    p_s = masked_softmax(s_s, valid_s[:, :, :, None, :])
    o_s = jnp.einsum('nqghk,nqgkd->nqghd', p_s.astype(v_sel.dtype), v_sel)
    dist_w = t - kw_pos[None, :]
    valid_w = ((kw_pos[None, :] >= 0) & (dist_w >= 0) & (dist_w <= NSA_WINDOW))[None, :, None, None, :]
    s_w = jnp.einsum('nqghd,nkgd->nqghk', q, kw).astype(jnp.float32) * scale + rel_bias_shared(rel_bias, dist_w, g)
    p_w = masked_softmax(s_w, valid_w)
    o_w = jnp.einsum('nqghk,nkgd->nqghd', p_w.astype(vw.dtype), vw)
    return gate[..., 0:1] * o_c + gate[..., 1:2] * o_s + gate[..., 2:3] * o_w


def nsa_project(h, w_in, b_gate):
    n, s, _ = h.shape
    parts = split_cols(h @ w_in, NSA_IN_SIZES)
    q = split_heads(parts[0], NSA_KV_HEADS)
    kv = [a.reshape(n, s, NSA_KV_HEADS, HEAD_DIM) for a in parts[1:7]]
    gate = jax.nn.sigmoid((parts[7] + b_gate).astype(jnp.float32)).astype(h.dtype)
    gate = gate.reshape(n, s, NSA_KV_HEADS, N_HEADS // NSA_KV_HEADS, 3)
    return q, kv, gate


def to_sel_layout(rows):
    n, l, g, d = rows.shape
    return jnp.transpose(rows.reshape(n, l // NSA_CMP_BLOCK, NSA_CMP_BLOCK, g, d), (0, 3, 1, 2, 4))


def nsa_prompt(h, w_in, b_gate, w_phi_k, w_phi_v, w_o, rel_bias):
    n, s, _ = h.shape
    q, (kc, vc, ks, vs, kw, vw), gate = nsa_project(h, w_in, b_gate)
    nb = s // NSA_CMP_BLOCK
    kc_sum, vc_sum = compress_blocks(kc, w_phi_k), compress_blocks(vc, w_phi_v)
    c_end = (jnp.arange(nb) + 1) * NSA_CMP_BLOCK - 1
    ks_t, vs_t = to_sel_layout(ks), to_sel_layout(vs)

    def step(args):
        j, qb, gb, kwb, vwb = args
        q_pos = j * Q_BLOCK + jnp.arange(Q_BLOCK)
        kw_pos = j * Q_BLOCK - NSA_WINDOW + jnp.arange(NSA_WINDOW + Q_BLOCK)
        return nsa_core(qb, q_pos, kc_sum, vc_sum, c_end, ks_t, vs_t, kwb, vwb, kw_pos, gb, rel_bias)

    o = from_blocks(lax.map(step, (jnp.arange(s // Q_BLOCK), to_blocks(q, Q_BLOCK), to_blocks(gate, Q_BLOCK),
                                   band_blocks(kw, NSA_WINDOW), band_blocks(vw, NSA_WINDOW))))
    wb = min(NSA_WINDOW, s)
    return o.reshape(n, s, -1) @ w_o, (kc, vc, ks, vs, kw[:, s - wb:], vw[:, s - wb:])


def nsa_sample(h, slot, pool_kc, pool_vc, pool_ks, pool_vs, buf_kw, buf_vw, page_table,
               w_in, b_gate, w_phi_k, w_phi_v, w_o, rel_bias):
    n, t_new, _ = h.shape
    past = page_table.shape[1] * PAGE_SIZE
    q, (kc, vc, ks, vs, kw, vw), gate = nsa_project(h, w_in, b_gate)
    total = past + t_new
    pad = (-total) % NSA_CMP_BLOCK
    nb = (total + pad) // NSA_CMP_BLOCK

    def full_rows(pool, new):
        r = jnp.concatenate([gather_pages(pool, slot, page_table), new], axis=1)
        return jnp.pad(r, ((0, 0), (0, pad), (0, 0), (0, 0)))

    kc_sum = compress_blocks(full_rows(pool_kc, kc), w_phi_k)
    vc_sum = compress_blocks(full_rows(pool_vc, vc), w_phi_v)
    c_end = (jnp.arange(nb) + 1) * NSA_CMP_BLOCK - 1
    ks_t, vs_t = to_sel_layout(full_rows(pool_ks, ks)), to_sel_layout(full_rows(pool_vs, vs))
    wb = buf_kw.shape[1]
    kw_all = jnp.concatenate([buf_kw, kw], axis=1)
    vw_all = jnp.concatenate([buf_vw, vw], axis=1)
    kw_pos = past - wb + jnp.arange(wb + t_new)

    def step(args):
        i, qb, gb = args
        q_pos = past + i + jnp.arange(1)
        return nsa_core(qb, q_pos, kc_sum, vc_sum, c_end, ks_t, vs_t, kw_all, vw_all, kw_pos, gb, rel_bias)

    o = from_blocks(lax.map(step, (jnp.arange(t_new), to_blocks(q, 1), to_blocks(gate, 1))))
    return o.reshape(n, t_new, -1) @ w_o, (kc, vc, ks, vs, kw_all[:, t_new:], vw_all[:, t_new:])


def sb_core(q, q_pos, k, v, k_pos):
    z = jnp.einsum('nqghd,nkgd->nqghk', q, k).astype(jnp.float32) * (q.shape[-1] ** -0.5)
    valid = (k_pos[None, :] < q_pos[:, None])[None, :, None, None, :]
    u = jnp.where(valid, jax.nn.log_sigmoid(-z), 0.0)
    cs = jnp.cumsum(u, axis=-1)
    log_a = jax.nn.log_sigmoid(z) + (cs[..., -1:] - cs)
    a = jnp.where(valid, jnp.exp(log_a), 0.0)
    return jnp.einsum('nqghk,nkgd->nqghd', a.astype(v.dtype), v)


def sb_project(h, w_in):
    n, s, _ = h.shape
    q, k, v = split_cols(h @ w_in, SB_IN_SIZES)
    return (split_heads(q, SB_KV_HEADS), k.reshape(n, s, SB_KV_HEADS, HEAD_DIM),
            v.reshape(n, s, SB_KV_HEADS, HEAD_DIM))


def sb_prompt(h, w_in, w_o):
    n, s, _ = h.shape
    q, k, v = sb_project(h, w_in)
    k_pos = jnp.arange(s)

    def step(args):
        j, qb = args
        return sb_core(qb, j * Q_BLOCK + jnp.arange(Q_BLOCK), k, v, k_pos)

    o = from_blocks(lax.map(step, (jnp.arange(s // Q_BLOCK), to_blocks(q, Q_BLOCK))))
    return o.reshape(n, s, -1) @ w_o, (k, v)


def sb_sample(h, slot, pool_k, pool_v, page_table, w_in, w_o):
    n, t_new, _ = h.shape
    past = page_table.shape[1] * PAGE_SIZE
    q, k, v = sb_project(h, w_in)
    k_all = jnp.concatenate([gather_pages(pool_k, slot, page_table), k], axis=1)
    v_all = jnp.concatenate([gather_pages(pool_v, slot, page_table), v], axis=1)
    k_pos = jnp.arange(past + t_new)

    def step(args):
        i, qb = args
        return sb_core(qb, past + i + jnp.arange(1), k_all, v_all, k_pos)

    o = from_blocks(lax.map(step, (jnp.arange(t_new), to_blocks(q, 1))))
    return o.reshape(n, t_new, -1) @ w_o, (k, v)


def swa_core(q, q_pos, k, v, k_pos, sinks, rel_bias):
    g, hg = q.shape[2], q.shape[3]
    dist = q_pos[:, None] - k_pos[None, :]
    valid = ((k_pos[None, :] >= 0) & (dist >= 0) & (dist <= SWA_WINDOW))[None, :, None, None, :]
    s = jnp.einsum('nqghd,nkgd->nqghk', q, k).astype(jnp.float32) * (q.shape[-1] ** -0.5)
    s = jnp.where(valid, s + rel_bias_shared(rel_bias, dist, g), NEG_INF)
    sink = jnp.broadcast_to(sinks.astype(jnp.float32).reshape(1, 1, g, hg, 1), s.shape[:-1] + (1,))
    p = jax.nn.softmax(jnp.concatenate([s, sink], axis=-1), axis=-1)[..., :-1]
    return jnp.einsum('nqghk,nkgd->nqghd', p.astype(v.dtype), v)


def swa_project(h, w_in):
    n, s, _ = h.shape
    q, k, v = split_cols(h @ w_in, SWA_IN_SIZES)
    return (split_heads(q, SWA_KV_HEADS), k.reshape(n, s, SWA_KV_HEADS, HEAD_DIM),
            v.reshape(n, s, SWA_KV_HEADS, HEAD_DIM))


def swa_prompt(h, w_in, sinks, w_o, rel_bias):
    n, s, _ = h.shape
    q, k, v = swa_project(h, w_in)

    def step(args):
        j, qb, kb, vb = args
        q_pos = j * Q_BLOCK + jnp.arange(Q_BLOCK)
        k_pos = j * Q_BLOCK - SWA_WINDOW + jnp.arange(SWA_WINDOW + Q_BLOCK)
        return swa_core(qb, q_pos, kb, vb, k_pos, sinks, rel_bias)

    o = from_blocks(lax.map(step, (jnp.arange(s // Q_BLOCK), to_blocks(q, Q_BLOCK),
                                   band_blocks(k, SWA_WINDOW), band_blocks(v, SWA_WINDOW))))
    wb = min(SWA_WINDOW, s)
    return o.reshape(n, s, -1) @ w_o, (k[:, s - wb:], v[:, s - wb:])


def swa_sample(h, past, buf_k, buf_v, w_in, sinks, w_o, rel_bias):
    n, t_new, _ = h.shape
    q, k, v = swa_project(h, w_in)
    wb = buf_k.shape[1]
    k_all = jnp.concatenate([buf_k, k], axis=1)
    v_all = jnp.concatenate([buf_v, v], axis=1)
    k_pos = past - wb + jnp.arange(wb + t_new)
    o = swa_core(q, past + jnp.arange(t_new), k_all, v_all, k_pos, sinks, rel_bias)
    return o.reshape(n, t_new, -1) @ w_o, (k_all[:, t_new:], v_all[:, t_new:])


def expert_dispatch(x, e_idx, e_w, w_gate, w_up, w_down):
    t, d = x.shape
    a = t * MOE_TOPK
    flat_e = e_idx.reshape(-1)
    order = jnp.argsort(flat_e)
    sorted_e = flat_e[order]
    counts = jnp.bincount(flat_e, length=MOE_EXPERTS)
    padded = (counts + MOE_BLOCK - 1) // MOE_BLOCK * MOE_BLOCK
    pad_end = jnp.cumsum(padded)
    dest = (pad_end - padded)[sorted_e] + jnp.arange(a) - (jnp.cumsum(counts) - counts)[sorted_e]
    n_blocks = (a + MOE_EXPERTS * (MOE_BLOCK - 1)) // MOE_BLOCK + 1
    token = order // MOE_TOPK
    buf = jnp.zeros((n_blocks * MOE_BLOCK, d), x.dtype).at[dest].set(x[token])
    block_e = jnp.minimum(jnp.searchsorted(pad_end, jnp.arange(n_blocks) * MOE_BLOCK, side='right'),
                          MOE_EXPERTS - 1)

    def run_block(args):
        xb, e = args
        return (jax.nn.silu(xb @ w_gate[e]) * (xb @ w_up[e])) @ w_down[e]

    out = lax.map(run_block, (buf.reshape(n_blocks, MOE_BLOCK, d), block_e)).reshape(-1, d)
    contrib = out[dest] * e_w.reshape(-1)[order][:, None].astype(out.dtype)
    return jnp.zeros_like(x).at[token].add(contrib)


def hier_moe(h, w_rg, b_rg, w_re, b_re, w_gate, w_up, w_down):
    n, s, d = h.shape
    x = h.reshape(-1, d)
    g_logits = (x @ w_rg).astype(jnp.float32) + b_rg.astype(jnp.float32)
    g_sel = jnp.argmax(g_logits, axis=-1)
    g_w = jnp.take_along_axis(jax.nn.softmax(g_logits, axis=-1), g_sel[:, None], axis=-1)
    e_logits = ((x @ w_re).astype(jnp.float32) + b_re.astype(jnp.float32)).reshape(-1, MOE_GROUPS, MOE_EXPERTS_PER_GROUP)
    e_in = jnp.take_along_axis(e_logits, g_sel[:, None, None], axis=1)[:, 0]
    top_v, top_i = lax.top_k(e_in, MOE_TOPK)
    e_w = jax.nn.softmax(top_v, axis=-1) * g_w
    e_idx = g_sel[:, None] * MOE_EXPERTS_PER_GROUP + top_i
    return expert_dispatch(x, e_idx, e_w, w_gate, w_up, w_down).reshape(n, s, d)


def setup_inputs(seed: int = 0) -> dict:
    key = jax.random.key(seed)
    keys = iter(jax.random.split(key, 64))

    def normal(shape, scale):
        return jax.random.normal(next(keys), shape, jnp.float32) * scale

    n_pages = PAST_LEN // PAGE_SIZE
    n_pool = (DEC_BATCH * n_pages * 5) // 4
    hd = HEAD_DIM
    hq = N_HEADS * HEAD_DIM
    nsa_win = min(NSA_WINDOW, PAST_LEN)
    swa_win = min(SWA_WINDOW, PAST_LEN)
    d_inv = D_MODEL ** -0.5
    page_table = jax.random.permutation(next(keys), n_pool)[: DEC_BATCH * n_pages]
    page_table = page_table.reshape(DEC_BATCH, n_pages).astype(jnp.int32)
    return {
        'x_prompt': normal((BATCH, SEQ, D_MODEL), 1.0),
        'x_sample': normal((DEC_BATCH, DEC_SEQ, D_MODEL), 1.0),
        'c_prompt': normal((BATCH, D_MODEL), 1.0),
        'c_sample': normal((DEC_BATCH, D_MODEL), 1.0),
        'cache_nsa_k_cmp': normal((N_LAYERS_A, n_pool, PAGE_SIZE, NSA_KV_HEADS, hd), 1.0),
        'cache_nsa_v_cmp': normal((N_LAYERS_A, n_pool, PAGE_SIZE, NSA_KV_HEADS, hd), 1.0),
        'cache_nsa_k_slc': normal((N_LAYERS_A, n_pool, PAGE_SIZE, NSA_KV_HEADS, hd), 1.0),
        'cache_nsa_v_slc': normal((N_LAYERS_A, n_pool, PAGE_SIZE, NSA_KV_HEADS, hd), 1.0),
        'cache_nsa_k_win': normal((N_LAYERS_A, DEC_BATCH, nsa_win, NSA_KV_HEADS, hd), 1.0),
        'cache_nsa_v_win': normal((N_LAYERS_A, DEC_BATCH, nsa_win, NSA_KV_HEADS, hd), 1.0),
        'cache_sb_k': normal((N_LAYERS_B, n_pool, PAGE_SIZE, SB_KV_HEADS, hd), 1.0),
        'cache_sb_v': normal((N_LAYERS_B, n_pool, PAGE_SIZE, SB_KV_HEADS, hd), 1.0),
        'cache_swa_k_win': normal((N_LAYERS_C, DEC_BATCH, swa_win, SWA_KV_HEADS, hd), 1.0),
        'cache_swa_v_win': normal((N_LAYERS_C, DEC_BATCH, swa_win, SWA_KV_HEADS, hd), 1.0),
        'page_table': page_table,
        'rel_bias': normal((REL_BUCKETS, N_HEADS), 0.5),
        'w_ada': normal((DEPTH, D_MODEL, 6 * D_MODEL), 0.5 * d_inv),
        'b_ada': normal((DEPTH, 6 * D_MODEL), 0.01),
        'ln1_g': 1.0 + normal((DEPTH, D_MODEL), 0.02),
        'ln1_b': normal((DEPTH, D_MODEL), 0.02),
        'ln2_g': 1.0 + normal((DEPTH, D_MODEL), 0.02),
        'ln2_b': normal((DEPTH, D_MODEL), 0.02),
        'w_in_nsa': normal((N_LAYERS_A, D_MODEL, sum(NSA_IN_SIZES)), d_inv),
        'b_gate_nsa': normal((N_LAYERS_A, 3 * N_HEADS), 0.1),
        'w_phi_k_nsa': normal((N_LAYERS_A, NSA_KV_HEADS, NSA_CMP_BLOCK, hd, hd), (NSA_CMP_BLOCK * hd) ** -0.5),
        'w_phi_v_nsa': normal((N_LAYERS_A, NSA_KV_HEADS, NSA_CMP_BLOCK, hd, hd), (NSA_CMP_BLOCK * hd) ** -0.5),
        'w_o_nsa': normal((N_LAYERS_A, hq, D_MODEL), DEEPNORM_BETA * hq ** -0.5),
        'w_in_sb': normal((N_LAYERS_B, D_MODEL, sum(SB_IN_SIZES)), d_inv),
        'w_o_sb': normal((N_LAYERS_B, hq, D_MODEL), DEEPNORM_BETA * hq ** -0.5),
        'w_in_swa': normal((N_LAYERS_C, D_MODEL, sum(SWA_IN_SIZES)), d_inv),
        'sinks_swa': normal((N_LAYERS_C, N_HEADS), 1.0),
        'w_o_swa': normal((N_LAYERS_C, hq, D_MODEL), DEEPNORM_BETA * hq ** -0.5),
        'w_route_group': normal((DEPTH, D_MODEL, MOE_GROUPS), d_inv),
        'b_route_group': normal((DEPTH, MOE_GROUPS), 0.01),
        'w_route_expert': normal((DEPTH, D_MODEL, MOE_EXPERTS), d_inv),
        'b_route_expert': normal((DEPTH, MOE_EXPERTS), 0.01),
        'w_exp_gate': normal((DEPTH, MOE_EXPERTS, D_MODEL, MOE_D_FF), d_inv),
        'w_exp_up': normal((DEPTH, MOE_EXPERTS, D_MODEL, MOE_D_FF), d_inv),
        'w_exp_down': normal((DEPTH, MOE_EXPERTS, MOE_D_FF, D_MODEL), DEEPNORM_BETA * MOE_D_FF ** -0.5),
    }


def reference(x_prompt, x_sample, c_prompt, c_sample, cache_nsa_k_cmp, cache_nsa_v_cmp, cache_nsa_k_slc,
              cache_nsa_v_slc, cache_nsa_k_win, cache_nsa_v_win, cache_sb_k, cache_sb_v, cache_swa_k_win,
              cache_swa_v_win, page_table, rel_bias, w_ada, b_ada, ln1_g, ln1_b, ln2_g, ln2_b, w_in_nsa,
              b_gate_nsa, w_phi_k_nsa, w_phi_v_nsa, w_o_nsa, w_in_sb, w_o_sb, w_in_swa, sinks_swa, w_o_swa,
              w_route_group, b_route_group, w_route_expert, b_route_expert, w_exp_gate, w_exp_up, w_exp_down):
    past = page_table.shape[1] * PAGE_SIZE
    xp, xs = x_prompt, x_sample
    st_a_p, st_a_s, st_b_p, st_b_s, st_c_p, st_c_s = [], [], [], [], [], []
    for layer in range(DEPTH):
        kind, slot = layer % N_MIXERS, layer // N_MIXERS
        mp = ada_modulation(c_prompt, w_ada[layer], b_ada[layer])
        ms = ada_modulation(c_sample, w_ada[layer], b_ada[layer])
        hp = xp * (1.0 + mp[1]) + mp[0]
        hs = xs * (1.0 + ms[1]) + ms[0]
        if kind == 0:
            yp, stp = nsa_prompt(hp, w_in_nsa[slot], b_gate_nsa[slot], w_phi_k_nsa[slot], w_phi_v_nsa[slot],
                                 w_o_nsa[slot], rel_bias)
            ys, sts = nsa_sample(hs, slot, cache_nsa_k_cmp, cache_nsa_v_cmp, cache_nsa_k_slc, cache_nsa_v_slc,
                                 cache_nsa_k_win[slot], cache_nsa_v_win[slot], page_table, w_in_nsa[slot],
                                 b_gate_nsa[slot], w_phi_k_nsa[slot], w_phi_v_nsa[slot], w_o_nsa[slot], rel_bias)
            st_a_p.append(stp)
            st_a_s.append(sts)
        elif kind == 1:
            yp, stp = sb_prompt(hp, w_in_sb[slot], w_o_sb[slot])
            ys, sts = sb_sample(hs, slot, cache_sb_k, cache_sb_v, page_table, w_in_sb[slot], w_o_sb[slot])
            st_b_p.append(stp)
            st_b_s.append(sts)
        else:
            yp, stp = swa_prompt(hp, w_in_swa[slot], sinks_swa[slot], w_o_swa[slot], rel_bias)
            ys, sts = swa_sample(hs, past, cache_swa_k_win[slot], cache_swa_v_win[slot], w_in_swa[slot],
                                 sinks_swa[slot], w_o_swa[slot], rel_bias)
            st_c_p.append(stp)
            st_c_s.append(sts)
        xp = layer_norm(DEEPNORM_ALPHA * xp + (1.0 + mp[2]) * yp, ln1_g[layer], ln1_b[layer])
        xs = layer_norm(DEEPNORM_ALPHA * xs + (1.0 + ms[2]) * ys, ln1_g[layer], ln1_b[layer])
        moe_w = (w_route_group[layer], b_route_group[layer], w_route_expert[layer], b_route_expert[layer],
                 w_exp_gate[layer], w_exp_up[layer], w_exp_down[layer])
        fp = hier_moe(xp * (1.0 + mp[4]) + mp[3], *moe_w)
        fs = hier_moe(xs * (1.0 + ms[4]) + ms[3], *moe_w)
        xp = layer_norm(DEEPNORM_ALPHA * xp + (1.0 + mp[5]) * fp, ln2_g[layer], ln2_b[layer])
        xs = layer_norm(DEEPNORM_ALPHA * xs + (1.0 + ms[5]) * fs, ln2_g[layer], ln2_b[layer])
    a_p = [jnp.stack([st[i] for st in st_a_p]) for i in range(6)]
    a_s = [jnp.stack([st[i] for st in st_a_s]) for i in range(6)]
    b_p = [jnp.stack([st[i] for st in st_b_p]) for i in range(2)]
    b_s = [jnp.stack([st[i] for st in st_b_s]) for i in range(2)]
    c_p = [jnp.stack([st[i] for st in st_c_p]) for i in range(2)]
    c_s = [jnp.stack([st[i] for st in st_c_s]) for i in range(2)]
    return (xp, xs, a_p[0], a_s[0], a_p[1], a_s[1], a_p[2], a_s[2], a_p[3], a_s[3], a_p[4], a_s[4], a_p[5], a_s[5],
            b_p[0], b_s[0], b_p[1], b_s[1], c_p[0], c_s[0], c_p[1], c_s[1])
```

```python
import functools
import math

import jax
import jax.numpy as jnp
from jax import lax
from jax.experimental import pallas as pl
from jax.experimental.pallas import tpu as pltpu

F32 = jnp.float32
BF16 = jnp.bfloat16

D_MODEL = 1024
DEPTH = 4
PAGE_SIZE = 128
N_HEADS = 16
HEAD_DIM = D_MODEL // N_HEADS
N_MIXERS = 3
Q_BLOCK = 128

NSA_KV_HEADS = 2
NSA_CMP_BLOCK = 64
NSA_TOPK = 16
NSA_WINDOW = 512
NSA_FORCED_SCORE = float(N_HEADS + 1)
SB_KV_HEADS = 4
SWA_KV_HEADS = 2
SWA_WINDOW = 128

REL_BUCKETS = 32
REL_MAX_DIST = 128

MOE_GROUPS = 4
MOE_EXPERTS_PER_GROUP = 8
MOE_EXPERTS = MOE_GROUPS * MOE_EXPERTS_PER_GROUP
MOE_TOPK = 2
MOE_D_FF = 512

DEEPNORM_ALPHA = (2 * DEPTH) ** 0.25
LN_EPS = 1e-5
NEG_INF = -1e30

VMEM_LIMIT_BYTES = 56 * 1024 * 1024
LANES = 128
ROW_TILE = 256
MOE_TILE = 256
PAGES_PER_STEP = 8

NSA_COLS = dict(kc=8, vc=9, ks=10, vs=11, kw=12, vw=13, gate=14)
NSA_NTOT = 15 * LANES
SB_NTOT = D_MODEL + 2 * SB_KV_HEADS * HEAD_DIM
SWA_COLS = dict(k=8, v=9)
SWA_NTOT = 10 * LANES


def _cparams(*sem):
    return pltpu.CompilerParams(dimension_semantics=sem, vmem_limit_bytes=VMEM_LIMIT_BYTES)


def _dot_nt(a, b):
    return lax.dot_general(a, b, (((1,), (1,)), ((), ())), preferred_element_type=F32)


def _mm_kernel(x_ref, w_ref, b_ref, o_ref, acc_ref, *, act, nk):
    k = pl.program_id(2)

    @pl.when(k == 0)
    def _():
        acc_ref[...] = jnp.zeros_like(acc_ref)

    x = x_ref[...].astype(F32)
    if act == "silu":
        x = x * jax.nn.sigmoid(x)
    acc_ref[...] += jnp.dot(x.astype(BF16), w_ref[...].astype(BF16), preferred_element_type=F32)

    @pl.when(k == nk - 1)
    def _():
        o_ref[...] = (acc_ref[...] + b_ref[...]).astype(o_ref.dtype)


def _mm(x, w, b=None, *, act=None, tm, tn, tk, m_rows=None, row_block_offset=0):
    m = x.shape[0] if m_rows is None else m_rows
    kdim, n = w.shape
    if b is None:
        b = jnp.zeros((n,), F32)
    b = b.reshape(1, n).astype(F32)
    nk = kdim // tk
    return pl.pallas_call(
        functools.partial(_mm_kernel, act=act, nk=nk),
        grid=(m // tm, n // tn, nk),
        in_specs=[
            pl.BlockSpec((tm, tk), lambda i, j, k: (i + row_block_offset, k)),
            pl.BlockSpec((tk, tn), lambda i, j, k: (k, j)),
            pl.BlockSpec((1, tn), lambda i, j, k: (0, j)),
        ],
        out_specs=pl.BlockSpec((tm, tn), lambda i, j, k: (i, j)),
        out_shape=jax.ShapeDtypeStruct((m, n), F32),
        scratch_shapes=[pltpu.VMEM((tm, tn), F32)],
        compiler_params=_cparams("parallel", "parallel", "arbitrary"),
        name="mm",
    )(x, w, b)


def _modmm_kernel(x_ref, sh_ref, sc_ref, w_ref, b_ref, o_ref, *h_refs):
    h = x_ref[...] * (1.0 + sc_ref[0]) + sh_ref[0]
    hb = h.astype(BF16)
    o_ref[...] = jnp.dot(hb, w_ref[...], preferred_element_type=F32) + b_ref[...]
    if h_refs:
        h_refs[0][...] = hb


def _mod_spec(mod, tm, rows_per_mod):
    r = mod.shape[1]
    if r == 1:
        tiles = rows_per_mod // tm
        return pl.BlockSpec((1, 1, D_MODEL), lambda i: (i // tiles, 0, 0))
    return pl.BlockSpec((1, tm, D_MODEL), lambda i: (i, 0, 0))


def _modmm(x, shift, scale, w_bf16, bias, rows_per_mod, *, emit_h=False, tm=ROW_TILE):
    t = x.shape[0]
    n = w_bf16.shape[1]
    out_shape = [jax.ShapeDtypeStruct((t, n), F32)]
    out_specs = [pl.BlockSpec((tm, n), lambda i: (i, 0))]
    if emit_h:
        out_shape.append(jax.ShapeDtypeStruct((t, D_MODEL), BF16))
        out_specs.append(pl.BlockSpec((tm, D_MODEL), lambda i: (i, 0)))
    res = pl.pallas_call(
        _modmm_kernel,
        grid=(t // tm,),
        in_specs=[
            pl.BlockSpec((tm, D_MODEL), lambda i: (i, 0)),
            _mod_spec(shift, tm, rows_per_mod),
            _mod_spec(scale, tm, rows_per_mod),
            pl.BlockSpec((D_MODEL, n), lambda i: (0, 0)),
            pl.BlockSpec((1, n), lambda i: (0, 0)),
        ],
        out_specs=out_specs,
        out_shape=out_shape,
        compiler_params=_cparams("parallel"),
        name="modmm",
    )(x, shift, scale, w_bf16, bias.reshape(1, n).astype(F32))
    return res if emit_h else res[0]


def _layer_norm_rows(r, g, b):
    mu = jnp.mean(r, axis=-1, keepdims=True)
    d = r - mu
    var = jnp.mean(d * d, axis=-1, keepdims=True)
    return d * lax.rsqrt(var + LN_EPS) * g + b


def _proj_res_ln_kernel(a_ref, x_ref, gt_ref, w_ref, g_ref, b_ref, o_ref):
    y = jnp.dot(a_ref[...], w_ref[...], preferred_element_type=F32)
    r = DEEPNORM_ALPHA * x_ref[...] + (1.0 + gt_ref[0]) * y
    o_ref[...] = _layer_norm_rows(r, g_ref[...], b_ref[...])


def _proj_res_ln(a_bf16, x, gate, w_bf16, ln_g, ln_b, rows_per_mod, *, tm=ROW_TILE):
    t = x.shape[0]
    return pl.pallas_call(
        _proj_res_ln_kernel,
        grid=(t // tm,),
        in_specs=[
            pl.BlockSpec((tm, D_MODEL), lambda i: (i, 0)),
            pl.BlockSpec((tm, D_MODEL), lambda i: (i, 0)),
            _mod_spec(gate, tm, rows_per_mod),
            pl.BlockSpec((D_MODEL, D_MODEL), lambda i: (0, 0)),
            pl.BlockSpec((1, D_MODEL), lambda i: (0, 0)),
            pl.BlockSpec((1, D_MODEL), lambda i: (0, 0)),
        ],
        out_specs=pl.BlockSpec((tm, D_MODEL), lambda i: (i, 0)),
        out_shape=jax.ShapeDtypeStruct((t, D_MODEL), F32),
        compiler_params=_cparams("parallel"),
        name="proj_res_ln",
    )(a_bf16, x, gate, w_bf16, ln_g.reshape(1, -1), ln_b.reshape(1, -1))


def _combine_res_ln_kernel(f0_ref, f1_ref, w_ref, x_ref, gt_ref, g_ref, b_ref, o_ref):
    w = w_ref[...]
    f = f0_ref[...] * w[:, 0:1] + f1_ref[...] * w[:, 1:2]
    r = DEEPNORM_ALPHA * x_ref[...] + (1.0 + gt_ref[0]) * f
    o_ref[...] = _layer_norm_rows(r, g_ref[...], b_ref[...])


def _combine_res_ln(f0, f1, w, x, gate, ln_g, ln_b, rows_per_mod, *, tm=ROW_TILE):
    t = x.shape[0]
    row = lambda i: (i, 0)
    return pl.pallas_call(
        _combine_res_ln_kernel,
        grid=(t // tm,),
        in_specs=[
            pl.BlockSpec((tm, D_MODEL), row),
            pl.BlockSpec((tm, D_MODEL), row),
            pl.BlockSpec((tm, MOE_TOPK), row),
            pl.BlockSpec((tm, D_MODEL), row),
            _mod_spec(gate, tm, rows_per_mod),
            pl.BlockSpec((1, D_MODEL), lambda i: (0, 0)),
            pl.BlockSpec((1, D_MODEL), lambda i: (0, 0)),
        ],
        out_specs=pl.BlockSpec((tm, D_MODEL), row),
        out_shape=jax.ShapeDtypeStruct((t, D_MODEL), F32),
        compiler_params=_cparams("parallel"),
        name="combine_res_ln",
    )(f0, f1, w, x, gate, ln_g.reshape(1, -1), ln_b.reshape(1, -1))


def _moe_ffn_kernel(te_ref, tv_ref, x_ref, wg_ref, wu_ref, wd_ref, o_ref, wg_b, wu_b, wd_b):
    i = pl.program_id(0)
    e = te_ref[i]
    prev = te_ref[jnp.maximum(i - 1, 0)]

    @pl.when((i == 0) | (e != prev))
    def _():
        wg_b[...] = wg_ref[0].astype(BF16)
        wu_b[...] = wu_ref[0].astype(BF16)
        wd_b[...] = wd_ref[0].astype(BF16)

    @pl.when(tv_ref[i] > 0)
    def _():
        x = x_ref[...]
        g = jnp.dot(x, wg_b[...], preferred_element_type=F32)
        u = jnp.dot(x, wu_b[...], preferred_element_type=F32)
        hmid = (g * jax.nn.sigmoid(g)) * u
        o_ref[...] = jnp.dot(hmid.astype(BF16), wd_b[...], preferred_element_type=F32)

    @pl.when(tv_ref[i] == 0)
    def _():
        o_ref[...] = jnp.zeros_like(o_ref)


def _moe_ffn(x_sorted, tile_e, tile_valid, w_gate, w_up, w_down, *, tm=MOE_TILE):
    p = x_sorted.shape[0]
    n_tiles = p // tm
    grid_spec = pltpu.PrefetchScalarGridSpec(
        num_scalar_prefetch=2,
        grid=(n_tiles,),
        in_specs=[
            pl.BlockSpec((tm, D_MODEL), lambda i, te, tv: (i, 0)),
            pl.BlockSpec((1, D_MODEL, MOE_D_FF), lambda i, te, tv: (te[i], 0, 0)),
            pl.BlockSpec((1, D_MODEL, MOE_D_FF), lambda i, te, tv: (te[i], 0, 0)),
            pl.BlockSpec((1, MOE_D_FF, D_MODEL), lambda i, te, tv: (te[i], 0, 0)),
        ],
        out_specs=pl.BlockSpec((tm, D_MODEL), lambda i, te, tv: (i, 0)),
        scratch_shapes=[
            pltpu.VMEM((D_MODEL, MOE_D_FF), BF16),
            pltpu.VMEM((D_MODEL, MOE_D_FF), BF16),
            pltpu.VMEM((MOE_D_FF, D_MODEL), BF16),
        ],
    )
    return pl.pallas_call(
        _moe_ffn_kernel,
        grid_spec=grid_spec,
        out_shape=jax.ShapeDtypeStruct((p, D_MODEL), F32),
        compiler_params=_cparams("arbitrary"),
        name="moe_ffn",
    )(tile_e, tile_valid, x_sorted, w_gate, w_up, w_down)


def _route(logits):
    g_logits = logits[:, :MOE_GROUPS]
    g_sel = jnp.argmax(g_logits, axis=-1)
    g_w = jnp.take_along_axis(jax.nn.softmax(g_logits, axis=-1), g_sel[:, None], axis=-1)
    e_logits = logits[:, MOE_GROUPS:MOE_GROUPS + MOE_EXPERTS].reshape(-1, MOE_GROUPS, MOE_EXPERTS_PER_GROUP)
    e_in = jnp.take_along_axis(e_logits, g_sel[:, None, None], axis=1)[:, 0]
    top_v, top_i = lax.top_k(e_in, MOE_TOPK)
    e_w = jax.nn.softmax(top_v, axis=-1) * g_w
    e_idx = (g_sel[:, None] * MOE_EXPERTS_PER_GROUP + top_i).astype(jnp.int32)
    return e_idx, e_w


def _moe_dispatch(e_idx, tm):
    t = e_idx.shape[0]
    a = t * MOE_TOPK
    flat_e = e_idx.reshape(-1)
    order = jnp.argsort(flat_e)
    sorted_e = flat_e[order]
    counts = jnp.bincount(flat_e, length=MOE_EXPERTS)
    padded = (counts + tm - 1) // tm * tm
    pad_end = jnp.cumsum(padded)
    dest = (pad_end - padded)[sorted_e] + jnp.arange(a) - (jnp.cumsum(counts) - counts)[sorted_e]
    n_tiles = (a + MOE_EXPERTS * (tm - 1)) // tm + 1
    src = jnp.full((n_tiles * tm,), t, jnp.int32).at[dest].set((order // MOE_TOPK).astype(jnp.int32))
    tile_start = jnp.arange(n_tiles) * tm
    tile_e = jnp.minimum(jnp.searchsorted(pad_end, tile_start, side="right"), MOE_EXPERTS - 1).astype(jnp.int32)
    tile_valid = (tile_start < pad_end[-1]).astype(jnp.int32)
    dest_by_assign = jnp.zeros((a,), jnp.int32).at[order].set(dest.astype(jnp.int32)).reshape(t, MOE_TOPK)
    return src, tile_e, tile_valid, dest_by_assign


def _rel_bucket(dist):
    n = jnp.maximum(dist, 0)
    exact = REL_BUCKETS // 2
    logv = jnp.log(jnp.maximum(n, 1).astype(F32) / exact) / math.log(REL_MAX_DIST / exact)
    large = jnp.minimum(exact + (logv * (REL_BUCKETS - exact)).astype(jnp.int32), REL_BUCKETS - 1)
    return jnp.where(n < exact, n, large)


def _bias_rows(rel_bias, dist, ok, q_len):
    b = jnp.transpose(rel_bias[_rel_bucket(dist)].astype(F32), (2, 0, 1))
    return jnp.where(ok[None], b, NEG_INF).reshape(N_HEADS * q_len, dist.shape[1])


def _band_tables(rel_bias, n_delta, window):
    qi = jnp.arange(Q_BLOCK)[:, None]
    ki = jnp.arange(Q_BLOCK)[None, :]
    tabs = []
    for delta in range(n_delta):
        d = delta * Q_BLOCK + qi - ki
        tabs.append(_bias_rows(rel_bias, d, (d >= 0) & (d <= window), Q_BLOCK))
    return jnp.stack(tabs)


def _build_qbd(q, n_groups):
    r = q.shape[0]
    hpg = N_HEADS // n_groups
    blocks = []
    for hh in range(N_HEADS):
        g = hh // hpg
        parts = []
        if g > 0:
            parts.append(jnp.zeros((r, g * HEAD_DIM), F32))
        parts.append(q[:, hh * HEAD_DIM:(hh + 1) * HEAD_DIM])
        if g < n_groups - 1:
            parts.append(jnp.zeros((r, (n_groups - 1 - g) * HEAD_DIM), F32))
        blocks.append(jnp.concatenate(parts, axis=1))
    return (jnp.concatenate(blocks, axis=0) * (HEAD_DIM ** -0.5)).astype(BF16)


def _flash_update(s, v_tile, m_ref, l_ref, acc_ref):
    m_prev = m_ref[...]
    m_new = jnp.maximum(m_prev, jnp.max(s, axis=1, keepdims=True))
    alpha = jnp.exp(m_prev - m_new)
    p = jnp.exp(s - m_new)
    l_ref[...] = alpha * l_ref[...] + jnp.sum(p, axis=1, keepdims=True)
    acc_ref[...] = alpha * acc_ref[...] + jnp.dot(p.astype(BF16), v_tile, preferred_element_type=F32)
    m_ref[...] = m_new


def _rank_select(score, cand, b_idx, n_blocks, k):
    rank = jnp.zeros(score.shape, F32)
    for bp in range(n_blocks):
        col = score[..., bp:bp + 1]
        tie = jnp.where(b_idx > bp, 1.0, 0.0)
        rank = rank + jnp.where(col > score, 1.0, jnp.where(col == score, tie, 0.0))
    return jnp.where(cand, jnp.where(rank < float(k), 1.0, 0.0), 0.0)


def _heads_to_lanes(read_rows, n_groups, r):
    hpg = N_HEADS // n_groups
    outs = []
    for hh in range(N_HEADS):
        g = hh // hpg
        outs.append(read_rows(hh)[:, g * HEAD_DIM:(g + 1) * HEAD_DIM])
    return jnp.concatenate(outs, axis=1)


def _nsa_prompt_kernel(q_ref, kcs_ref, vcs_ref, ks_ref, vs_ref, kw_ref, vw_ref, gl_ref, bg_ref,
                       tab_ref, bc_ref, ex_ref, o_ref,
                       m_ref, l_ref, acc_ref, oc_ref, os_ref, sel_ref, *, nb):
    j = pl.program_id(1)
    g_n = NSA_KV_HEADS
    hpg = N_HEADS // g_n
    rows = N_HEADS * Q_BLOCK
    qbd = _build_qbd(q_ref[...], g_n)

    s_c = _dot_nt(qbd, kcs_ref[...].astype(BF16)) + bc_ref[0]
    qi = lax.broadcasted_iota(jnp.int32, (Q_BLOCK, nb), 0)
    bi = lax.broadcasted_iota(jnp.int32, (Q_BLOCK, nb), 1)
    t = j * Q_BLOCK + qi
    valid_c = (t - (bi + 1) * NSA_CMP_BLOCK + 1) >= 0
    e_c = jnp.exp(s_c - jnp.max(s_c, axis=1, keepdims=True))
    p_c = e_c / jnp.sum(e_c, axis=1, keepdims=True)
    p_c4 = jnp.where(valid_c[None, None], p_c.reshape(g_n, hpg, Q_BLOCK, nb), 0.0)
    oc_ref[...] = jnp.dot(p_c4.reshape(rows, nb).astype(BF16), vcs_ref[...].astype(BF16),
                          preferred_element_type=F32)

    imp = jnp.sum(p_c4, axis=1)
    cur = t // NSA_CMP_BLOCK
    forced = (bi == cur) | (bi == cur - 1) | (bi == 0)
    cand = bi <= cur
    score = jnp.where(cand[None], jnp.where(forced[None], NSA_FORCED_SCORE, imp), -1.0)
    bi3 = lax.broadcasted_iota(jnp.int32, (g_n, Q_BLOCK, nb), 2)
    sel = _rank_select(score, jnp.broadcast_to(cand[None], score.shape), bi3, nb, min(NSA_TOPK, nb))
    sel_ref[...] = sel.astype(BF16)

    def reset():
        m_ref[...] = jnp.full(m_ref.shape, NEG_INF, F32)
        l_ref[...] = jnp.zeros(l_ref.shape, F32)
        acc_ref[...] = jnp.zeros(acc_ref.shape, F32)

    reset()

    def sel_body(kt, carry):
        off = pl.multiple_of(kt * Q_BLOCK, Q_BLOCK)
        k_t = ks_ref[pl.ds(off, Q_BLOCK), :].astype(BF16)
        v_t = vs_ref[pl.ds(off, Q_BLOCK), :].astype(BF16)
        s = _dot_nt(qbd, k_t) + tab_ref[jnp.minimum(j - kt, 2)]
        e_t = ex_ref[kt]
        msk = jnp.stack([jnp.dot(sel_ref[g], e_t, preferred_element_type=F32) for g in range(g_n)])
        madd = (1.0 - msk) * NEG_INF
        s = (s.reshape(g_n, hpg, Q_BLOCK, Q_BLOCK) + madd[:, None]).reshape(rows, Q_BLOCK)
        _flash_update(s, v_t, m_ref, l_ref, acc_ref)
        return carry

    lax.fori_loop(0, j + 1, sel_body, 0)
    os_ref[...] = acc_ref[...] / l_ref[...]

    reset()

    def win_body(i, carry):
        off = pl.multiple_of((j - i) * Q_BLOCK, Q_BLOCK)
        k_t = kw_ref[pl.ds(off, Q_BLOCK), :].astype(BF16)
        v_t = vw_ref[pl.ds(off, Q_BLOCK), :].astype(BF16)
        s = _dot_nt(qbd, k_t) + tab_ref[i]
        _flash_update(s, v_t, m_ref, l_ref, acc_ref)
        return carry

    lax.fori_loop(0, jnp.minimum(j, NSA_WINDOW // Q_BLOCK) + 1, win_body, 0)

    sig = jax.nn.sigmoid(gl_ref[...] + bg_ref[...])

    def head_out(hh):
        r0 = hh * Q_BLOCK
        o_w = acc_ref[r0:r0 + Q_BLOCK, :] / l_ref[r0:r0 + Q_BLOCK, :]
        return (sig[:, 3 * hh:3 * hh + 1] * oc_ref[r0:r0 + Q_BLOCK, :]
                + sig[:, 3 * hh + 1:3 * hh + 2] * os_ref[r0:r0 + Q_BLOCK, :]
                + sig[:, 3 * hh + 2:3 * hh + 3] * o_w)

    o_ref[...] = _heads_to_lanes(head_out, g_n, Q_BLOCK).astype(o_ref.dtype)


def _nsa_prompt_attn(y, kcs, vcs, b_gate_row, tabs, bc, ex, n_batch, seq):
    nq = seq // Q_BLOCK
    nb = seq // NSA_CMP_BLOCK
    rows = N_HEADS * Q_BLOCK
    c = NSA_COLS
    kv = lambda col: pl.BlockSpec((seq, LANES), lambda n, j: (n, col))
    return pl.pallas_call(
        functools.partial(_nsa_prompt_kernel, nb=nb),
        grid=(n_batch, nq),
        in_specs=[
            pl.BlockSpec((Q_BLOCK, D_MODEL), lambda n, j: (n * nq + j, 0)),
            pl.BlockSpec((nb, LANES), lambda n, j: (n, 0)),
            pl.BlockSpec((nb, LANES), lambda n, j: (n, 0)),
            kv(c["ks"]), kv(c["vs"]), kv(c["kw"]), kv(c["vw"]),
            pl.BlockSpec((Q_BLOCK, LANES), lambda n, j: (n * nq + j, c["gate"])),
            pl.BlockSpec((1, LANES), lambda n, j: (0, 0)),
            pl.BlockSpec((NSA_WINDOW // Q_BLOCK + 1, rows, Q_BLOCK), lambda n, j: (0, 0, 0)),
            pl.BlockSpec((1, rows, nb), lambda n, j: (j, 0, 0)),
            pl.BlockSpec((nq, nb, Q_BLOCK), lambda n, j: (0, 0, 0)),
        ],
        out_specs=pl.BlockSpec((Q_BLOCK, D_MODEL), lambda n, j: (n * nq + j, 0)),
        out_shape=jax.ShapeDtypeStruct((n_batch * seq, D_MODEL), BF16),
        scratch_shapes=[
            pltpu.VMEM((rows, LANES), F32), pltpu.VMEM((rows, LANES), F32), pltpu.VMEM((rows, LANES), F32),
            pltpu.VMEM((rows, LANES), F32), pltpu.VMEM((rows, LANES), F32),
            pltpu.VMEM((NSA_KV_HEADS, Q_BLOCK, nb), BF16),
        ],
        compiler_params=_cparams("parallel", "arbitrary"),
        name="nsa_prompt",
    )(y, kcs, vcs, y, y, y, y, y, b_gate_row, tabs, bc, ex)


def _log_sigmoid(z):
    return jnp.minimum(z, 0.0) - jnp.log1p(jnp.exp(-jnp.abs(z)))


def _suffix_sums(u, tri):
    u_hi = u.astype(BF16)
    u_lo = (u - u_hi.astype(F32)).astype(BF16)
    return (jnp.dot(u_hi, tri, preferred_element_type=F32) + jnp.dot(u_lo, tri, preferred_element_type=F32))


def _sb_prompt_kernel(q_ref, k_ref, v_ref, vm_ref, tri_ref, o_ref, acc_ref, c_ref):
    j = pl.program_id(1)
    g_n = SB_KV_HEADS
    qbd = _build_qbd(q_ref[...], g_n)
    acc_ref[...] = jnp.zeros(acc_ref.shape, F32)
    c_ref[...] = jnp.zeros(c_ref.shape, F32)

    def body(i, carry):
        off = pl.multiple_of((j - i) * Q_BLOCK, Q_BLOCK)
        k_t = k_ref[pl.ds(off, Q_BLOCK), :].astype(BF16)
        v_t = v_ref[pl.ds(off, Q_BLOCK), :].astype(BF16)
        z = _dot_nt(qbd, k_t)
        vm = vm_ref[jnp.minimum(i, 1)]
        ls = _log_sigmoid(z)
        u = (ls - z) * vm
        log_a = ls + _suffix_sums(u, tri_ref[...]) + c_ref[...]
        a = jnp.exp(log_a) * vm
        acc_ref[...] += jnp.dot(a.astype(BF16), v_t, preferred_element_type=F32)
        c_ref[...] += jnp.sum(u, axis=1, keepdims=True)
        return carry

    lax.fori_loop(0, j + 1, body, 0)
    o_ref[...] = _heads_to_lanes(lambda hh: acc_ref[hh * Q_BLOCK:(hh + 1) * Q_BLOCK, :], g_n,
                                 Q_BLOCK).astype(o_ref.dtype)


def _sb_tables():
    qi = jnp.arange(Q_BLOCK)[:, None]
    ki = jnp.arange(Q_BLOCK)[None, :]
    diag = jnp.tile((ki < qi).astype(F32), (N_HEADS, 1))
    vm = jnp.stack([diag, jnp.ones_like(diag)])
    tri = (qi > ki).astype(BF16)
    return vm, tri


def _sb_prompt_attn(y, n_batch, seq):
    nq = seq // Q_BLOCK
    rows = N_HEADS * Q_BLOCK
    kvw = SB_KV_HEADS * HEAD_DIM
    vm, tri = _sb_tables()
    return pl.pallas_call(
        _sb_prompt_kernel,
        grid=(n_batch, nq),
        in_specs=[
            pl.BlockSpec((Q_BLOCK, D_MODEL), lambda n, j: (n * nq + j, 0)),
            pl.BlockSpec((seq, kvw), lambda n, j: (n, D_MODEL // kvw)),
            pl.BlockSpec((seq, kvw), lambda n, j: (n, D_MODEL // kvw + 1)),
            pl.BlockSpec((2, rows, Q_BLOCK), lambda n, j: (0, 0, 0)),
            pl.BlockSpec((Q_BLOCK, Q_BLOCK), lambda n, j: (0, 0)),
        ],
        out_specs=pl.BlockSpec((Q_BLOCK, D_MODEL), lambda n, j: (n * nq + j, 0)),
        out_shape=jax.ShapeDtypeStruct((n_batch * seq, D_MODEL), BF16),
        scratch_shapes=[pltpu.VMEM((rows, kvw), F32), pltpu.VMEM((rows, Q_BLOCK), F32)],
        compiler_params=_cparams("parallel", "arbitrary"),
        name="sb_prompt",
    )(y, y, y, vm, tri)


def _swa_prompt_kernel(q_ref, k_ref, v_ref, tab_ref, sink_ref, o_ref, m_ref, l_ref, acc_ref):
    j = pl.program_id(1)
    qbd = _build_qbd(q_ref[...], SWA_KV_HEADS)
    m_ref[...] = sink_ref[...]
    l_ref[...] = jnp.ones(l_ref.shape, F32)
    acc_ref[...] = jnp.zeros(acc_ref.shape, F32)

    def body(i, carry):
        off = pl.multiple_of((j - i) * Q_BLOCK, Q_BLOCK)
        k_t = k_ref[pl.ds(off, Q_BLOCK), :].astype(BF16)
        v_t = v_ref[pl.ds(off, Q_BLOCK), :].astype(BF16)
        s = _dot_nt(qbd, k_t) + tab_ref[i]
        _flash_update(s, v_t, m_ref, l_ref, acc_ref)
        return carry

    lax.fori_loop(0, jnp.minimum(j, SWA_WINDOW // Q_BLOCK) + 1, body, 0)

    def head_out(hh):
        r0 = hh * Q_BLOCK
        return acc_ref[r0:r0 + Q_BLOCK, :] / l_ref[r0:r0 + Q_BLOCK, :]

    o_ref[...] = _heads_to_lanes(head_out, SWA_KV_HEADS, Q_BLOCK).astype(o_ref.dtype)


def _swa_prompt_attn(y, tabs, sink_rows, n_batch, seq):
    nq = seq // Q_BLOCK
    rows = N_HEADS * Q_BLOCK
    c = SWA_COLS
    return pl.pallas_call(
        _swa_prompt_kernel,
        grid=(n_batch, nq),
        in_specs=[
            pl.BlockSpec((Q_BLOCK, D_MODEL), lambda n, j: (n * nq + j, 0)),
            pl.BlockSpec((seq, LANES), lambda n, j: (n, c["k"])),
            pl.BlockSpec((seq, LANES), lambda n, j: (n, c["v"])),
            pl.BlockSpec((SWA_WINDOW // Q_BLOCK + 1, rows, Q_BLOCK), lambda n, j: (0, 0, 0)),
            pl.BlockSpec((rows, LANES), lambda n, j: (0, 0)),
        ],
        out_specs=pl.BlockSpec((Q_BLOCK, D_MODEL), lambda n, j: (n * nq + j, 0)),
        out_shape=jax.ShapeDtypeStruct((n_batch * seq, D_MODEL), BF16),
        scratch_shapes=[pltpu.VMEM((rows, LANES), F32)] * 3,
        compiler_params=_cparams("parallel", "arbitrary"),
        name="swa_prompt",
    )(y, y, y, tabs, sink_rows)


def _pad_rows(x, n):
    return jnp.concatenate([x, jnp.zeros((n - x.shape[0], x.shape[1]), x.dtype)], axis=0)


def _nsa_sample_kernel(pt_ref, q_ref, ksn_ref, vsn_ref, kwn_ref, vwn_ref, gl_ref, bg_ref,
                       kcs_ref, vcs_ref, bcs_ref, tlast_ref, tfar_ref, tnew_ref, wtb_ref, ex_ref,
                       kwb_ref, vwb_ref, *rest, ppc, nchunk, nb, t_new):
    k_pages = rest[:ppc]
    v_pages = rest[ppc:2 * ppc]
    o_ref = rest[2 * ppc]
    m_ref, l_ref, acc_ref, oc_ref, ow_ref, sel_ref = rest[2 * ppc + 1:]
    c = pl.program_id(1)
    g_n = NSA_KV_HEADS
    hpg = N_HEADS // g_n
    rows = N_HEADS * t_new
    nbp = kcs_ref.shape[0]
    qbd = _build_qbd(q_ref[...], g_n)

    @pl.when(c == 0)
    def _():
        s_c = _dot_nt(qbd, kcs_ref[...].astype(BF16)) + bcs_ref[...]
        bi = lax.broadcasted_iota(jnp.int32, (rows, nbp), 1)
        e_c = jnp.exp(s_c - jnp.max(s_c, axis=1, keepdims=True))
        p_c = jnp.where(bi < nb - 1, e_c / jnp.sum(e_c, axis=1, keepdims=True), 0.0)
        oc_ref[...] = jnp.dot(p_c.astype(BF16), vcs_ref[...].astype(BF16), preferred_element_type=F32)
        imp = jnp.sum(p_c.reshape(g_n, hpg, t_new, nbp), axis=1)
        b3 = lax.broadcasted_iota(jnp.int32, (g_n, t_new, nbp), 2)
        forced = (b3 == nb - 1) | (b3 == nb - 2) | (b3 == 0)
        cand = b3 <= nb - 1
        score = jnp.where(cand, jnp.where(forced, NSA_FORCED_SCORE, imp), -1.0)
        sel = _rank_select(score, cand, b3, nb, min(NSA_TOPK, nb))
        sel_ref[...] = jnp.broadcast_to(sel[:, None], (g_n, hpg, t_new, nbp)).reshape(rows, nbp).astype(BF16)

        s_b = _dot_nt(qbd, kwb_ref[...].astype(BF16)) + wtb_ref[...]
        s_n = _dot_nt(qbd, _pad_rows(kwn_ref[...], LANES).astype(BF16)) + tnew_ref[...]
        m_w = jnp.maximum(jnp.max(s_b, axis=1, keepdims=True), jnp.max(s_n, axis=1, keepdims=True))
        p_b = jnp.exp(s_b - m_w)
        p_n = jnp.exp(s_n - m_w)
        l_w = jnp.sum(p_b, axis=1, keepdims=True) + jnp.sum(p_n, axis=1, keepdims=True)
        o_w = (jnp.dot(p_b.astype(BF16), vwb_ref[...].astype(BF16), preferred_element_type=F32)
               + jnp.dot(p_n.astype(BF16), _pad_rows(vwn_ref[...], LANES).astype(BF16),
                         preferred_element_type=F32))
        ow_ref[...] = o_w / l_w

        s0 = _dot_nt(qbd, _pad_rows(ksn_ref[...], LANES).astype(BF16)) + tnew_ref[...]
        m0 = jnp.max(s0, axis=1, keepdims=True)
        p0 = jnp.exp(s0 - m0)
        m_ref[...] = jnp.broadcast_to(m0, m_ref.shape)
        l_ref[...] = jnp.broadcast_to(jnp.sum(p0, axis=1, keepdims=True), l_ref.shape)
        acc_ref[...] = jnp.dot(p0.astype(BF16), _pad_rows(vsn_ref[...], LANES).astype(BF16),
                               preferred_element_type=F32)

    msk = jnp.dot(sel_ref[...], ex_ref[...], preferred_element_type=F32)
    parts = []
    for r in range(ppc):
        s_r = _dot_nt(qbd, k_pages[r][...].astype(BF16))
        if r == ppc - 1:
            s_r = s_r + jnp.where(c == nchunk - 1, tlast_ref[...], tfar_ref[...])
        else:
            s_r = s_r + tfar_ref[...]
        parts.append(s_r)
    s = jnp.concatenate(parts, axis=1) + (1.0 - msk) * NEG_INF
    m_prev = m_ref[...]
    m_new = jnp.maximum(m_prev, jnp.max(s, axis=1, keepdims=True))
    alpha = jnp.exp(m_prev - m_new)
    p = jnp.exp(s - m_new[:, 0:1])
    pv = jnp.dot(p[:, 0:PAGE_SIZE].astype(BF16), v_pages[0][...].astype(BF16), preferred_element_type=F32)
    for r in range(1, ppc):
        pv = pv + jnp.dot(p[:, r * PAGE_SIZE:(r + 1) * PAGE_SIZE].astype(BF16),
                          v_pages[r][...].astype(BF16), preferred_element_type=F32)
    l_ref[...] = alpha * l_ref[...] + jnp.sum(p, axis=1, keepdims=True)
    acc_ref[...] = alpha * acc_ref[...] + pv
    m_ref[...] = m_new

    @pl.when(c == nchunk - 1)
    def _():
        sig = jax.nn.sigmoid(gl_ref[...] + bg_ref[...])

        def head_out(hh):
            r0 = hh * t_new
            o_s = acc_ref[r0:r0 + t_new, :] / l_ref[r0:r0 + t_new, :]
            return (sig[:, 3 * hh:3 * hh + 1] * oc_ref[r0:r0 + t_new, :]
                    + sig[:, 3 * hh + 1:3 * hh + 2] * o_s
                    + sig[:, 3 * hh + 2:3 * hh + 3] * ow_ref[r0:r0 + t_new, :])

        o_ref[...] = _heads_to_lanes(head_out, g_n, t_new).astype(o_ref.dtype)


def _nsa_sample_tables(rel_bias, past, t_new, wb, nb, nbp):
    i = jnp.arange(t_new)[:, None]
    ki = jnp.arange(LANES)[None, :]
    d_new = i - ki
    tnew = _bias_rows(rel_bias, d_new, (d_new >= 0) & (ki < t_new), t_new)
    d_last = PAGE_SIZE + i - ki
    tlast = _bias_rows(rel_bias, d_last, d_last >= 0, t_new)
    d_far = jnp.full((t_new, LANES), 2 * PAGE_SIZE)
    tfar = _bias_rows(rel_bias, d_far, d_far >= 0, t_new)
    r = jnp.arange(wb)[None, :]
    d_w = wb + i - r
    wtb = _bias_rows(rel_bias, d_w, (d_w <= NSA_WINDOW) & (past - wb + r >= 0), t_new)
    b = jnp.arange(nbp)[None, :]
    d_c = past + i - ((b + 1) * NSA_CMP_BLOCK - 1)
    bcs = _bias_rows(rel_bias, d_c, (d_c >= 0) & (b < nb), t_new)
    return tnew, tlast, tfar, wtb, bcs


def _nsa_sample_attn(y, kcs, vcs, b_gate_row, tables, ex, kwb, vwb, pool_k, pool_v, pt_flat,
                     n_seq, t_new, n_pages, nb):
    ppc = PAGES_PER_STEP
    nchunk = n_pages // ppc
    rows = N_HEADS * t_new
    nbp = kcs.shape[1]
    wb = kwb.shape[1]
    tnew, tlast, tfar, wtb, bcs = tables
    c = NSA_COLS
    new = lambda col: pl.BlockSpec((t_new, LANES), lambda n, ch, pt: (n, col))
    const = lambda shape: pl.BlockSpec(shape, lambda n, ch, pt: (0,) * len(shape))

    def page_spec(r):
        return pl.BlockSpec((None, PAGE_SIZE, LANES),
                            lambda n, ch, pt: (pt[n * n_pages + ch * ppc + r], 0, 0))

    in_specs = [
        pl.BlockSpec((t_new, D_MODEL), lambda n, ch, pt: (n, 0)),
        new(c["ks"]), new(c["vs"]), new(c["kw"]), new(c["vw"]), new(c["gate"]),
        const((1, LANES)),
        pl.BlockSpec((None, nbp, LANES), lambda n, ch, pt: (n, 0, 0)),
        pl.BlockSpec((None, nbp, LANES), lambda n, ch, pt: (n, 0, 0)),
        const((rows, nbp)), const((rows, LANES)), const((rows, LANES)), const((rows, LANES)),
        const((rows, wb)),
        pl.BlockSpec((nbp, ppc * PAGE_SIZE), lambda n, ch, pt: (0, ch)),
        pl.BlockSpec((None, wb, LANES), lambda n, ch, pt: (n, 0, 0)),
        pl.BlockSpec((None, wb, LANES), lambda n, ch, pt: (n, 0, 0)),
    ] + [page_spec(r) for r in range(ppc)] + [page_spec(r) for r in range(ppc)]
    grid_spec = pltpu.PrefetchScalarGridSpec(
        num_scalar_prefetch=1,
        grid=(n_seq, nchunk),
        in_specs=in_specs,
        out_specs=pl.BlockSpec((t_new, D_MODEL), lambda n, ch, pt: (n, 0)),
        scratch_shapes=[pltpu.VMEM((rows, LANES), F32)] * 5 + [pltpu.VMEM((rows, nbp), BF16)],
    )
    return pl.pallas_call(
        functools.partial(_nsa_sample_kernel, ppc=ppc, nchunk=nchunk, nb=nb, t_new=t_new),
        grid_spec=grid_spec,
        out_shape=jax.ShapeDtypeStruct((n_seq * t_new, D_MODEL), BF16),
        compiler_params=_cparams("parallel", "arbitrary"),
        name="nsa_sample",
    )(pt_flat, y, y, y, y, y, y, b_gate_row, kcs, vcs, bcs, tlast, tfar, tnew, wtb, ex, kwb, vwb,
      *([pool_k] * ppc), *([pool_v] * ppc))


def _sb_sample_kernel(pt_ref, q_ref, kn_ref, vn_ref, vmn_ref, tri_ref, *rest, ppc, t_new):
    k_pages = rest[:ppc]
    v_pages = rest[ppc:2 * ppc]
    o_ref = rest[2 * ppc]
    acc_ref, c_ref = rest[2 * ppc + 1:]
    c = pl.program_id(1)
    g_n = SB_KV_HEADS
    qbd = _build_qbd(q_ref[...], g_n)

    def tile(k_t, v_t, vm):
        z = _dot_nt(qbd, k_t)
        ls = _log_sigmoid(z)
        u = ls - z if vm is None else (ls - z) * vm
        log_a = ls + _suffix_sums(u, tri_ref[...]) + c_ref[...]
        a = jnp.exp(log_a) if vm is None else jnp.exp(log_a) * vm
        acc_ref[...] += jnp.dot(a.astype(BF16), v_t, preferred_element_type=F32)
        c_ref[...] += jnp.sum(u, axis=1, keepdims=True)

    @pl.when(c == 0)
    def _():
        acc_ref[...] = jnp.zeros(acc_ref.shape, F32)
        c_ref[...] = jnp.zeros(c_ref.shape, F32)
        tile(_pad_rows(kn_ref[...], PAGE_SIZE).astype(BF16), _pad_rows(vn_ref[...], PAGE_SIZE).astype(BF16),
             vmn_ref[...])

    for r in reversed(range(ppc)):
        tile(k_pages[r][...].astype(BF16), v_pages[r][...].astype(BF16), None)

    @pl.when(c == pl.num_programs(1) - 1)
    def _():
        o_ref[...] = _heads_to_lanes(lambda hh: acc_ref[hh * t_new:(hh + 1) * t_new, :], g_n,
                                     t_new).astype(o_ref.dtype)


def _sb_sample_attn(y, pool_k, pool_v, pt_flat, n_seq, t_new, n_pages):
    ppc = PAGES_PER_STEP
    nchunk = n_pages // ppc
    rows = N_HEADS * t_new
    kvw = SB_KV_HEADS * HEAD_DIM
    i = jnp.arange(t_new)[:, None]
    ki = jnp.arange(PAGE_SIZE)[None, :]
    vmn = jnp.tile((ki < i).astype(F32), (N_HEADS, 1))
    _, tri = _sb_tables()

    def page_spec(r):
        return pl.BlockSpec((None, PAGE_SIZE, kvw),
                            lambda n, ch, pt: (pt[n * n_pages + (nchunk - 1 - ch) * ppc + r], 0, 0))

    in_specs = [
        pl.BlockSpec((t_new, D_MODEL), lambda n, ch, pt: (n, 0)),
        pl.BlockSpec((t_new, kvw), lambda n, ch, pt: (n, D_MODEL // kvw)),
        pl.BlockSpec((t_new, kvw), lambda n, ch, pt: (n, D_MODEL // kvw + 1)),
        pl.BlockSpec((rows, PAGE_SIZE), lambda n, ch, pt: (0, 0)),
        pl.BlockSpec((PAGE_SIZE, PAGE_SIZE), lambda n, ch, pt: (0, 0)),
    ] + [page_spec(r) for r in range(ppc)] + [page_spec(r) for r in range(ppc)]
    grid_spec = pltpu.PrefetchScalarGridSpec(
        num_scalar_prefetch=1,
        grid=(n_seq, nchunk),
        in_specs=in_specs,
        out_specs=pl.BlockSpec((t_new, D_MODEL), lambda n, ch, pt: (n, 0)),
        scratch_shapes=[pltpu.VMEM((rows, kvw), F32), pltpu.VMEM((rows, PAGE_SIZE), F32)],
    )
    return pl.pallas_call(
        functools.partial(_sb_sample_kernel, ppc=ppc, t_new=t_new),
        grid_spec=grid_spec,
        out_shape=jax.ShapeDtypeStruct((n_seq * t_new, D_MODEL), BF16),
        compiler_params=_cparams("parallel", "arbitrary"),
        name="sb_sample",
    )(pt_flat, y, y, y, vmn, tri, *([pool_k] * ppc), *([pool_v] * ppc))


def _swa_sample_kernel(q_ref, kn_ref, vn_ref, kb_ref, vb_ref, tb_ref, tn_ref, sink_ref, o_ref, *, t_new):
    qbd = _build_qbd(q_ref[...], SWA_KV_HEADS)
    s_b = _dot_nt(qbd, kb_ref[...].astype(BF16)) + tb_ref[...]
    s_n = _dot_nt(qbd, _pad_rows(kn_ref[...], LANES).astype(BF16)) + tn_ref[...]
    sink = sink_ref[...][:, 0:1]
    m = jnp.maximum(jnp.maximum(jnp.max(s_b, axis=1, keepdims=True), jnp.max(s_n, axis=1, keepdims=True)), sink)
    p_b = jnp.exp(s_b - m)
    p_n = jnp.exp(s_n - m)
    l = jnp.sum(p_b, axis=1, keepdims=True) + jnp.sum(p_n, axis=1, keepdims=True) + jnp.exp(sink - m)
    o = (jnp.dot(p_b.astype(BF16), vb_ref[...].astype(BF16), preferred_element_type=F32)
         + jnp.dot(p_n.astype(BF16), _pad_rows(vn_ref[...], LANES).astype(BF16),
                   preferred_element_type=F32)) / l
    o_ref[...] = _heads_to_lanes(lambda hh: o[hh * t_new:(hh + 1) * t_new, :], SWA_KV_HEADS,
                                 t_new).astype(o_ref.dtype)


def _swa_sample_attn(y, kb, vb, rel_bias, sinks, past, n_seq, t_new):
    rows = N_HEADS * t_new
    wb = kb.shape[1]
    i = jnp.arange(t_new)[:, None]
    r = jnp.arange(wb)[None, :]
    d_b = wb + i - r
    tb = _bias_rows(rel_bias, d_b, (d_b <= SWA_WINDOW) & (past - wb + r >= 0), t_new)
    ki = jnp.arange(LANES)[None, :]
    d_n = i - ki
    tn = _bias_rows(rel_bias, d_n, (d_n >= 0) & (ki < t_new), t_new)
    sink_rows = jnp.broadcast_to(jnp.repeat(sinks.astype(F32), t_new)[:, None], (rows, LANES))
    c = SWA_COLS
    return pl.pallas_call(
        functools.partial(_swa_sample_kernel, t_new=t_new),
        grid=(n_seq,),
        in_specs=[
            pl.BlockSpec((t_new, D_MODEL), lambda n: (n, 0)),
            pl.BlockSpec((t_new, LANES), lambda n: (n, c["k"])),
            pl.BlockSpec((t_new, LANES), lambda n: (n, c["v"])),
            pl.BlockSpec((None, wb, LANES), lambda n: (n, 0, 0)),
            pl.BlockSpec((None, wb, LANES), lambda n: (n, 0, 0)),
            pl.BlockSpec((rows, wb), lambda n: (0, 0)),
            pl.BlockSpec((rows, LANES), lambda n: (0, 0)),
            pl.BlockSpec((rows, LANES), lambda n: (0, 0)),
        ],
        out_specs=pl.BlockSpec((t_new, D_MODEL), lambda n: (n, 0)),
        out_shape=jax.ShapeDtypeStruct((n_seq * t_new, D_MODEL), BF16),
        compiler_params=_cparams("parallel"),
        name="swa_sample",
    )(y, y, y, kb, vb, tb, tn, sink_rows)


def _pad_cols(w, n):
    return jnp.pad(w, ((0, 0), (0, n - w.shape[1])))


def _compress_weight(w_phi):
    g_n = w_phi.shape[0]
    wp = jnp.transpose(w_phi, (1, 0, 2, 3))
    return jnp.einsum("jgde,gh->jgdhe", wp, jnp.eye(g_n, dtype=w_phi.dtype)).reshape(
        NSA_CMP_BLOCK * g_n * HEAD_DIM, g_n * HEAD_DIM)


def _expand_blocks(n_blocks_padded, n_keys):
    b = jnp.arange(n_blocks_padded)[:, None]
    k = jnp.arange(n_keys)[None, :]
    return (b == k // NSA_CMP_BLOCK).astype(BF16)


def _nsa_layer(xp, xs, mod_p, mod_s, slot, n_batch, seq, n_seq, t_new, pools, bufs, page_table, params,
               rel_bias, band_tabs):
    w_in, b_gate, w_phi_k, w_phi_v = params
    kvw = NSA_KV_HEADS * HEAD_DIM
    w = _pad_cols(w_in, NSA_NTOT).astype(BF16)
    bias = jnp.zeros((NSA_NTOT,), F32)
    yp = _modmm(xp, mod_p[0], mod_p[1], w, bias, seq)
    ys = _modmm(xs, mod_s[0], mod_s[1], w, bias, t_new)
    b_gate_row = jnp.pad(b_gate.astype(F32), (0, LANES - b_gate.shape[0])).reshape(1, LANES)
    wk = _compress_weight(w_phi_k)
    wv = _compress_weight(w_phi_v)
    col = lambda y, name: y[:, NSA_COLS[name] * LANES:(NSA_COLS[name] + 1) * LANES]

    nq = seq // Q_BLOCK
    nb = seq // NSA_CMP_BLOCK
    kc_p, vc_p = col(yp, "kc"), col(yp, "vc")
    flat_k = NSA_CMP_BLOCK * kvw
    kcs_p = _mm(kc_p.reshape(n_batch * nb, flat_k), wk, tm=min(256, n_batch * nb), tn=kvw, tk=2048)
    vcs_p = _mm(vc_p.reshape(n_batch * nb, flat_k), wv, tm=min(256, n_batch * nb), tn=kvw, tk=2048)
    t = jnp.arange(seq)[:, None]
    c_end = (jnp.arange(nb)[None, :] + 1) * NSA_CMP_BLOCK - 1
    d_c = t - c_end
    bc = jnp.transpose(rel_bias[_rel_bucket(d_c)].astype(F32), (2, 0, 1))
    bc = jnp.where((d_c >= 0)[None], bc, NEG_INF).reshape(N_HEADS, nq, Q_BLOCK, nb)
    bc = jnp.transpose(bc, (1, 0, 2, 3)).reshape(nq, N_HEADS * Q_BLOCK, nb)
    ex = _expand_blocks(nb, seq).reshape(nb, nq, Q_BLOCK).transpose(1, 0, 2)
    op = _nsa_prompt_attn(yp, kcs_p, vcs_p, b_gate_row, band_tabs, bc, ex, n_batch, seq)

    n_pages = page_table.shape[1]
    past = n_pages * PAGE_SIZE
    pool_kc, pool_vc, pool_ks, pool_vs = pools
    n_pool = pool_kc.shape[1]
    nb_s = past // NSA_CMP_BLOCK + 1
    nbp = (nb_s + 15) // 16 * 16
    blocks_per_page = PAGE_SIZE // NSA_CMP_BLOCK
    tm_pool = next(c for c in (512, 256, 128, 64, 32, 16, 8) if (n_pool * blocks_per_page) % c == 0)

    def summaries(pool, new_rows, wmat):
        flat = pool.reshape(-1, flat_k)
        rows_per_slot = n_pool * blocks_per_page
        s_all = _mm(flat, wmat, tm=tm_pool, tn=kvw, tk=2048, m_rows=rows_per_slot,
                    row_block_offset=slot * (rows_per_slot // tm_pool))
        s_all = s_all.reshape(n_pool, blocks_per_page, kvw)
        past_sum = s_all[page_table].reshape(n_seq, n_pages * blocks_per_page, kvw)
        new_flat = new_rows.reshape(n_seq, t_new * kvw)
        last = _mm(new_flat, wmat[:t_new * kvw], tm=n_seq, tn=kvw, tk=t_new * kvw)
        pad = jnp.zeros((n_seq, nbp - nb_s, kvw), F32)
        return jnp.concatenate([past_sum, last[:, None, :], pad], axis=1)

    kcs_s = summaries(pool_kc, col(ys, "kc"), wk)
    vcs_s = summaries(pool_vc, col(ys, "vc"), wv)
    buf_kw, buf_vw = bufs
    wb = buf_kw.shape[1]
    tables = _nsa_sample_tables(rel_bias, past, t_new, wb, nb_s, nbp)
    ex_s = _expand_blocks(nbp, past)
    pt_flat = (page_table + slot * n_pool).reshape(-1).astype(jnp.int32)
    os_ = _nsa_sample_attn(ys, kcs_s, vcs_s, b_gate_row, tables, ex_s,
                           buf_kw.reshape(n_seq, wb, kvw), buf_vw.reshape(n_seq, wb, kvw),
                           pool_ks.reshape(-1, PAGE_SIZE, kvw), pool_vs.reshape(-1, PAGE_SIZE, kvw),
                           pt_flat, n_seq, t_new, n_pages, nb_s)

    shp_p = (n_batch, seq, NSA_KV_HEADS, HEAD_DIM)
    shp_s = (n_seq, t_new, NSA_KV_HEADS, HEAD_DIM)
    wbp = min(NSA_WINDOW, seq)
    st_p, st_s = [], []
    for name in ("kc", "vc", "ks", "vs"):
        st_p.append(col(yp, name).reshape(shp_p))
        st_s.append(col(ys, name).reshape(shp_s))
    for name, buf in (("kw", buf_kw), ("vw", buf_vw)):
        st_p.append(col(yp, name).reshape(shp_p)[:, seq - wbp:])
        st_s.append(jnp.concatenate([buf, col(ys, name).reshape(shp_s)], axis=1)[:, t_new:])
    return op, os_, st_p, st_s


def _sb_layer(xp, xs, mod_p, mod_s, slot, n_batch, seq, n_seq, t_new, pools, page_table, w_in):
    kvw = SB_KV_HEADS * HEAD_DIM
    w = w_in.astype(BF16)
    bias = jnp.zeros((SB_NTOT,), F32)
    yp = _modmm(xp, mod_p[0], mod_p[1], w, bias, seq)
    ys = _modmm(xs, mod_s[0], mod_s[1], w, bias, t_new)
    op = _sb_prompt_attn(yp, n_batch, seq)
    pool_k, pool_v = pools
    n_pool = pool_k.shape[1]
    n_pages = page_table.shape[1]
    pt_flat = (page_table + slot * n_pool).reshape(-1).astype(jnp.int32)
    os_ = _sb_sample_attn(ys, pool_k.reshape(-1, PAGE_SIZE, kvw), pool_v.reshape(-1, PAGE_SIZE, kvw),
                          pt_flat, n_seq, t_new, n_pages)
    shp_p = (n_batch, seq, SB_KV_HEADS, HEAD_DIM)
    shp_s = (n_seq, t_new, SB_KV_HEADS, HEAD_DIM)
    st_p = [yp[:, D_MODEL:D_MODEL + kvw].reshape(shp_p), yp[:, D_MODEL + kvw:].reshape(shp_p)]
    st_s = [ys[:, D_MODEL:D_MODEL + kvw].reshape(shp_s), ys[:, D_MODEL + kvw:].reshape(shp_s)]
    return op, os_, st_p, st_s


def _swa_layer(xp, xs, mod_p, mod_s, n_batch, seq, n_seq, t_new, bufs, past, w_in, sinks, rel_bias):
    kvw = SWA_KV_HEADS * HEAD_DIM
    w = w_in.astype(BF16)
    bias = jnp.zeros((SWA_NTOT,), F32)
    yp = _modmm(xp, mod_p[0], mod_p[1], w, bias, seq)
    ys = _modmm(xs, mod_s[0], mod_s[1], w, bias, t_new)
    tabs = _band_tables(rel_bias, SWA_WINDOW // Q_BLOCK + 1, SWA_WINDOW)
    sink_rows = jnp.broadcast_to(jnp.repeat(sinks.astype(F32), Q_BLOCK)[:, None], (N_HEADS * Q_BLOCK, LANES))
    op = _swa_prompt_attn(yp, tabs, sink_rows, n_batch, seq)
    buf_k, buf_v = bufs
    wb = buf_k.shape[1]
    os_ = _swa_sample_attn(ys, buf_k.reshape(n_seq, wb, kvw), buf_v.reshape(n_seq, wb, kvw), rel_bias, sinks,
                           past, n_seq, t_new)
    shp_p = (n_batch, seq, SWA_KV_HEADS, HEAD_DIM)
    shp_s = (n_seq, t_new, SWA_KV_HEADS, HEAD_DIM)
    wbp = min(SWA_WINDOW, seq)
    col = lambda y, name: y[:, SWA_COLS[name] * LANES:(SWA_COLS[name] + 1) * LANES]
    st_p = [col(yp, n).reshape(shp_p)[:, seq - wbp:] for n in ("k", "v")]
    st_s = [jnp.concatenate([b, col(ys, n).reshape(shp_s)], axis=1)[:, t_new:]
            for n, b in (("k", buf_k), ("v", buf_v))]
    return op, os_, st_p, st_s


def _moe_layer(xp, xs, mod_p, mod_s, seq, t_new, w_rg, b_rg, w_re, b_re, w_gate, w_up, w_down,
               ln_g, ln_b):
    tp, ts = xp.shape[0], xs.shape[0]
    w_r = _pad_cols(jnp.concatenate([w_rg, w_re], axis=1), LANES).astype(BF16)
    b_r = jnp.pad(jnp.concatenate([b_rg, b_re]).astype(F32), (0, LANES - MOE_GROUPS - MOE_EXPERTS))
    lg_p, h_p = _modmm(xp, mod_p[0], mod_p[1], w_r, b_r, seq, emit_h=True)
    lg_s, h_s = _modmm(xs, mod_s[0], mod_s[1], w_r, b_r, t_new, emit_h=True)
    e_idx, e_w = _route(jnp.concatenate([lg_p, lg_s], axis=0))
    src, tile_e, tile_valid, dest = _moe_dispatch(e_idx, MOE_TILE)
    h_all = jnp.concatenate([h_p, h_s, jnp.zeros((8, D_MODEL), BF16)], axis=0)
    out = _moe_ffn(h_all[src], tile_e, tile_valid, w_gate, w_up, w_down)
    f0, f1 = out[dest[:, 0]], out[dest[:, 1]]
    xp2 = _combine_res_ln(f0[:tp], f1[:tp], e_w[:tp], xp, mod_p[2], ln_g, ln_b, seq)
    xs2 = _combine_res_ln(f0[tp:], f1[tp:], e_w[tp:], xs, mod_s[2], ln_g, ln_b, t_new)
    return xp2, xs2


def kernel(x_prompt, x_sample, c_prompt, c_sample, cache_nsa_k_cmp, cache_nsa_v_cmp, cache_nsa_k_slc,
           cache_nsa_v_slc, cache_nsa_k_win, cache_nsa_v_win, cache_sb_k, cache_sb_v, cache_swa_k_win,
           cache_swa_v_win, page_table, rel_bias, w_ada, b_ada, ln1_g, ln1_b, ln2_g, ln2_b, w_in_nsa,
           b_gate_nsa, w_phi_k_nsa, w_phi_v_nsa, w_o_nsa, w_in_sb, w_o_sb, w_in_swa, sinks_swa, w_o_swa,
           w_route_group, b_route_group, w_route_expert, b_route_expert, w_exp_gate, w_exp_up, w_exp_down):
    n_batch, seq, _ = x_prompt.shape
    n_seq, t_new, _ = x_sample.shape
    past = page_table.shape[1] * PAGE_SIZE
    xp = x_prompt.reshape(n_batch * seq, D_MODEL)
    xs = x_sample.reshape(n_seq * t_new, D_MODEL)
    n_c = n_batch + n_seq
    n_c_pad = (n_c + 7) // 8 * 8
    c_all = jnp.concatenate([c_prompt, c_sample, jnp.zeros((n_c_pad - n_c, D_MODEL), F32)], axis=0)
    nsa_tabs = _band_tables(rel_bias, NSA_WINDOW // Q_BLOCK + 1, NSA_WINDOW)
    ts_tiles = (n_seq * t_new) // ROW_TILE

    st = {k: ([], []) for k in ("a", "b", "c")}
    for layer in range(DEPTH):
        kind, slot = layer % N_MIXERS, layer // N_MIXERS
        mods = _mm(c_all, w_ada[layer], b_ada[layer], act="silu", tm=n_c_pad, tn=1536, tk=D_MODEL)
        mp = [mods[:n_batch, k * D_MODEL:(k + 1) * D_MODEL].reshape(n_batch, 1, D_MODEL) for k in range(6)]
        ms = [jnp.repeat(mods[n_batch:n_c, k * D_MODEL:(k + 1) * D_MODEL], t_new, axis=0)
              .reshape(ts_tiles, ROW_TILE, D_MODEL) for k in range(6)]
        if kind == 0:
            op, os_, stp, sts = _nsa_layer(
                xp, xs, mp[0:2], ms[0:2], slot, n_batch, seq, n_seq, t_new,
                (cache_nsa_k_cmp, cache_nsa_v_cmp, cache_nsa_k_slc, cache_nsa_v_slc),
                (cache_nsa_k_win[slot], cache_nsa_v_win[slot]), page_table,
                (w_in_nsa[slot], b_gate_nsa[slot], w_phi_k_nsa[slot], w_phi_v_nsa[slot]), rel_bias, nsa_tabs)
            w_o, key = w_o_nsa[slot], "a"
        elif kind == 1:
            op, os_, stp, sts = _sb_layer(xp, xs, mp[0:2], ms[0:2], slot, n_batch, seq, n_seq, t_new,
                                          (cache_sb_k, cache_sb_v), page_table, w_in_sb[slot])
            w_o, key = w_o_sb[slot], "b"
        else:
            op, os_, stp, sts = _swa_layer(xp, xs, mp[0:2], ms[0:2], n_batch, seq, n_seq, t_new,
                                           (cache_swa_k_win[slot], cache_swa_v_win[slot]), past,
                                           w_in_swa[slot], sinks_swa[slot], rel_bias)
            w_o, key = w_o_swa[slot], "c"
        st[key][0].append(stp)
        st[key][1].append(sts)
        w_o = w_o.astype(BF16)
        xp = _proj_res_ln(op, xp, mp[2], w_o, ln1_g[layer], ln1_b[layer], seq)
        xs = _proj_res_ln(os_, xs, ms[2], w_o, ln1_g[layer], ln1_b[layer], t_new)
        xp, xs = _moe_layer(xp, xs, mp[3:6], ms[3:6], seq, t_new, w_route_group[layer], b_route_group[layer],
                            w_route_expert[layer], b_route_expert[layer], w_exp_gate[layer], w_exp_up[layer],
                            w_exp_down[layer], ln2_g[layer], ln2_b[layer])

    outs = [xp.reshape(n_batch, seq, D_MODEL), xs.reshape(n_seq, t_new, D_MODEL)]
    for key, n_state in (("a", 6), ("b", 2), ("c", 2)):
        for i in range(n_state):
            outs.append(jnp.stack([s[i] for s in st[key][0]]))
            outs.append(jnp.stack([s[i] for s in st[key][1]]))
    return tuple(outs)
```

```python
import functools
import math

import jax
import jax.numpy as jnp
from jax import lax
from jax.experimental import pallas as pl
from jax.experimental.pallas import tpu as pltpu

F32 = jnp.float32
BF16 = jnp.bfloat16

D_MODEL = 1024
DEPTH = 4
PAGE_SIZE = 128
N_HEADS = 16
HEAD_DIM = D_MODEL // N_HEADS
N_MIXERS = 3
Q_BLOCK = 128

NSA_KV_HEADS = 2
NSA_CMP_BLOCK = 64
NSA_TOPK = 16
NSA_WINDOW = 512
NSA_FORCED_SCORE = float(N_HEADS + 1)
SB_KV_HEADS = 4
SWA_KV_HEADS = 2
SWA_WINDOW = 128

REL_BUCKETS = 32
REL_MAX_DIST = 128

MOE_GROUPS = 4
MOE_EXPERTS_PER_GROUP = 8
MOE_EXPERTS = MOE_GROUPS * MOE_EXPERTS_PER_GROUP
MOE_TOPK = 2
MOE_D_FF = 512

DEEPNORM_ALPHA = (2 * DEPTH) ** 0.25
LN_EPS = 1e-5
NEG_INF = -1e30

VMEM_LIMIT_BYTES = 56 * 1024 * 1024
LANES = 128
ROW_TILE = 256
MOE_TILE = 256
PAGES_PER_STEP = 8
POOL_PAGES_PER_STEP = 128

NSA_COLS = dict(kc=8, vc=9, ks=10, vs=11, kw=12, vw=13, gate=14)
NSA_NTOT = 15 * LANES
SB_NTOT = D_MODEL + 2 * SB_KV_HEADS * HEAD_DIM
SWA_COLS = dict(k=8, v=9)
SWA_NTOT = 10 * LANES


def _cparams(*sem):
    return pltpu.CompilerParams(dimension_semantics=sem, vmem_limit_bytes=VMEM_LIMIT_BYTES)


def _dot_nt(a, b):
    return lax.dot_general(a, b, (((1,), (1,)), ((), ())), preferred_element_type=F32)


def _mm_kernel(x_ref, w_ref, b_ref, o_ref, acc_ref, *, act, nk):
    k = pl.program_id(2)

    @pl.when(k == 0)
    def _():
        acc_ref[...] = jnp.zeros_like(acc_ref)

    x = x_ref[...].astype(F32)
    if act == "silu":
        x = x * jax.nn.sigmoid(x)
    acc_ref[...] += jnp.dot(x.astype(BF16), w_ref[...].astype(BF16), preferred_element_type=F32)

    @pl.when(k == nk - 1)
    def _():
        o_ref[...] = (acc_ref[...] + b_ref[...]).astype(o_ref.dtype)


def _mm(x, w, b=None, *, act=None, tm, tn, tk):
    m = x.shape[0]
    kdim, n = w.shape
    if b is None:
        b = jnp.zeros((n,), F32)
    b = b.reshape(1, n).astype(F32)
    nk = kdim // tk
    return pl.pallas_call(
        functools.partial(_mm_kernel, act=act, nk=nk),
        grid=(m // tm, n // tn, nk),
        in_specs=[
            pl.BlockSpec((tm, tk), lambda i, j, k: (i, k)),
            pl.BlockSpec((tk, tn), lambda i, j, k: (k, j)),
            pl.BlockSpec((1, tn), lambda i, j, k: (0, j)),
        ],
        out_specs=pl.BlockSpec((tm, tn), lambda i, j, k: (i, j)),
        out_shape=jax.ShapeDtypeStruct((m, n), F32),
        scratch_shapes=[pltpu.VMEM((tm, tn), F32)],
        compiler_params=_cparams("parallel", "parallel", "arbitrary"),
        name="mm",
    )(x, w, b)


def _modmm_kernel(x_ref, sh_ref, sc_ref, w_ref, b_ref, o_ref, x_ref2, *, second):
    h = x_ref[...] * (1.0 + sc_ref[0]) + sh_ref[0]
    y = jnp.dot(h.astype(BF16), w_ref[...], preferred_element_type=F32) + b_ref[...]
    o_ref[...] = y
    if second == "h":
        x_ref2[...] = h
    else:
        x_ref2[...] = y.astype(BF16)


def _mod_spec(mod, tm, rows_per_mod):
    r = mod.shape[1]
    if r == 1:
        tiles = rows_per_mod // tm
        return pl.BlockSpec((1, 1, D_MODEL), lambda i: (i // tiles, 0, 0))
    return pl.BlockSpec((1, tm, D_MODEL), lambda i: (i, 0, 0))


def _modmm(x, shift, scale, w_bf16, bias, rows_per_mod, *, second, tm=ROW_TILE):
    t = x.shape[0]
    n = w_bf16.shape[1]
    out_shape = [jax.ShapeDtypeStruct((t, n), F32)]
    out_specs = [pl.BlockSpec((tm, n), lambda i: (i, 0))]
    if second == "h":
        out_shape.append(jax.ShapeDtypeStruct((t, D_MODEL), F32))
        out_specs.append(pl.BlockSpec((tm, D_MODEL), lambda i: (i, 0)))
    else:
        out_shape.append(jax.ShapeDtypeStruct((t, n), BF16))
        out_specs.append(pl.BlockSpec((tm, n), lambda i: (i, 0)))
    return pl.pallas_call(
        functools.partial(_modmm_kernel, second=second),
        grid=(t // tm,),
        in_specs=[
            pl.BlockSpec((tm, D_MODEL), lambda i: (i, 0)),
            _mod_spec(shift, tm, rows_per_mod),
            _mod_spec(scale, tm, rows_per_mod),
            pl.BlockSpec((D_MODEL, n), lambda i: (0, 0)),
            pl.BlockSpec((1, n), lambda i: (0, 0)),
        ],
        out_specs=out_specs,
        out_shape=out_shape,
        compiler_params=_cparams("parallel"),
        name="modmm",
    )(x, shift, scale, w_bf16, bias.reshape(1, n).astype(F32))


def _layer_norm_rows(r, g, b):
    mu = jnp.mean(r, axis=-1, keepdims=True)
    d = r - mu
    var = jnp.mean(d * d, axis=-1, keepdims=True)
    return d * lax.rsqrt(var + LN_EPS) * g + b


def _proj_res_ln_kernel(a_ref, x_ref, gt_ref, w_ref, g_ref, b_ref, o_ref):
    y = jnp.dot(a_ref[...], w_ref[...], preferred_element_type=F32)
    r = DEEPNORM_ALPHA * x_ref[...] + (1.0 + gt_ref[0]) * y
    o_ref[...] = _layer_norm_rows(r, g_ref[...], b_ref[...])


def _proj_res_ln(a_bf16, x, gate, w_bf16, ln_g, ln_b, rows_per_mod, *, tm=ROW_TILE):
    t = x.shape[0]
    return pl.pallas_call(
        _proj_res_ln_kernel,
        grid=(t // tm,),
        in_specs=[
            pl.BlockSpec((tm, D_MODEL), lambda i: (i, 0)),
            pl.BlockSpec((tm, D_MODEL), lambda i: (i, 0)),
            _mod_spec(gate, tm, rows_per_mod),
            pl.BlockSpec((D_MODEL, D_MODEL), lambda i: (0, 0)),
            pl.BlockSpec((1, D_MODEL), lambda i: (0, 0)),
            pl.BlockSpec((1, D_MODEL), lambda i: (0, 0)),
        ],
        out_specs=pl.BlockSpec((tm, D_MODEL), lambda i: (i, 0)),
        out_shape=jax.ShapeDtypeStruct((t, D_MODEL), F32),
        compiler_params=_cparams("parallel"),
        name="proj_res_ln",
    )(a_bf16, x, gate, w_bf16, ln_g.reshape(1, -1), ln_b.reshape(1, -1))


def _combine_res_ln_kernel(f0_ref, f1_ref, w_ref, x_ref, gt_ref, g_ref, b_ref, o_ref):
    w = w_ref[...]
    f = f0_ref[...] * w[:, 0:1] + f1_ref[...] * w[:, 1:2]
    r = DEEPNORM_ALPHA * x_ref[...] + (1.0 + gt_ref[0]) * f
    o_ref[...] = _layer_norm_rows(r, g_ref[...], b_ref[...])


def _combine_res_ln(f0, f1, w, x, gate, ln_g, ln_b, rows_per_mod, *, tm=ROW_TILE):
    t = x.shape[0]
    row = lambda i: (i, 0)
    return pl.pallas_call(
        _combine_res_ln_kernel,
        grid=(t // tm,),
        in_specs=[
            pl.BlockSpec((tm, D_MODEL), row),
            pl.BlockSpec((tm, D_MODEL), row),
            pl.BlockSpec((tm, MOE_TOPK), row),
            pl.BlockSpec((tm, D_MODEL), row),
            _mod_spec(gate, tm, rows_per_mod),
            pl.BlockSpec((1, D_MODEL), lambda i: (0, 0)),
            pl.BlockSpec((1, D_MODEL), lambda i: (0, 0)),
        ],
        out_specs=pl.BlockSpec((tm, D_MODEL), row),
        out_shape=jax.ShapeDtypeStruct((t, D_MODEL), F32),
        compiler_params=_cparams("parallel"),
        name="combine_res_ln",
    )(f0, f1, w, x, gate, ln_g.reshape(1, -1), ln_b.reshape(1, -1))


def _moe_ffn_kernel(te_ref, tv_ref, x_ref, wg_ref, wu_ref, wd_ref, o_ref, wg_b, wu_b, wd_b):
    i = pl.program_id(0)
    e = te_ref[i]
    prev = te_ref[jnp.maximum(i - 1, 0)]

    @pl.when((i == 0) | (e != prev))
    def _():
        wg_b[...] = wg_ref[0].astype(BF16)
        wu_b[...] = wu_ref[0].astype(BF16)
        wd_b[...] = wd_ref[0].astype(BF16)

    @pl.when(tv_ref[i] > 0)
    def _():
        x = x_ref[...].astype(BF16)
        g = jnp.dot(x, wg_b[...], preferred_element_type=F32)
        u = jnp.dot(x, wu_b[...], preferred_element_type=F32)
        hmid = (g * jax.nn.sigmoid(g)) * u
        o_ref[...] = jnp.dot(hmid.astype(BF16), wd_b[...], preferred_element_type=F32)

    @pl.when(tv_ref[i] == 0)
    def _():
        o_ref[...] = jnp.zeros_like(o_ref)


def _moe_ffn(x_sorted, tile_e, tile_valid, w_gate, w_up, w_down, *, tm=MOE_TILE):
    p = x_sorted.shape[0]
    n_tiles = p // tm
    grid_spec = pltpu.PrefetchScalarGridSpec(
        num_scalar_prefetch=2,
        grid=(n_tiles,),
        in_specs=[
            pl.BlockSpec((tm, D_MODEL), lambda i, te, tv: (i, 0)),
            pl.BlockSpec((1, D_MODEL, MOE_D_FF), lambda i, te, tv: (te[i], 0, 0)),
            pl.BlockSpec((1, D_MODEL, MOE_D_FF), lambda i, te, tv: (te[i], 0, 0)),
            pl.BlockSpec((1, MOE_D_FF, D_MODEL), lambda i, te, tv: (te[i], 0, 0)),
        ],
        out_specs=pl.BlockSpec((tm, D_MODEL), lambda i, te, tv: (i, 0)),
        scratch_shapes=[
            pltpu.VMEM((D_MODEL, MOE_D_FF), BF16),
            pltpu.VMEM((D_MODEL, MOE_D_FF), BF16),
            pltpu.VMEM((MOE_D_FF, D_MODEL), BF16),
        ],
    )
    return pl.pallas_call(
        _moe_ffn_kernel,
        grid_spec=grid_spec,
        out_shape=jax.ShapeDtypeStruct((p, D_MODEL), F32),
        compiler_params=_cparams("arbitrary"),
        name="moe_ffn",
    )(tile_e, tile_valid, x_sorted, w_gate, w_up, w_down)


def _route(logits):
    g_logits = logits[:, :MOE_GROUPS]
    g_sel = jnp.argmax(g_logits, axis=-1)
    g_w = jnp.take_along_axis(jax.nn.softmax(g_logits, axis=-1), g_sel[:, None], axis=-1)
    e_logits = logits[:, MOE_GROUPS:MOE_GROUPS + MOE_EXPERTS].reshape(-1, MOE_GROUPS, MOE_EXPERTS_PER_GROUP)
    e_in = jnp.take_along_axis(e_logits, g_sel[:, None, None], axis=1)[:, 0]
    top_v, top_i = lax.top_k(e_in, MOE_TOPK)
    e_w = jax.nn.softmax(top_v, axis=-1) * g_w
    e_idx = (g_sel[:, None] * MOE_EXPERTS_PER_GROUP + top_i).astype(jnp.int32)
    return e_idx, e_w


def _moe_dispatch(e_idx, tm):
    t = e_idx.shape[0]
    a = t * MOE_TOPK
    flat_e = e_idx.reshape(-1)
    order = jnp.argsort(flat_e)
    sorted_e = flat_e[order]
    counts = jnp.bincount(flat_e, length=MOE_EXPERTS)
    padded = (counts + tm - 1) // tm * tm
    pad_end = jnp.cumsum(padded)
    dest = (pad_end - padded)[sorted_e] + jnp.arange(a) - (jnp.cumsum(counts) - counts)[sorted_e]
    n_tiles = (a + MOE_EXPERTS * (tm - 1)) // tm + 1
    src = jnp.full((n_tiles * tm,), t, jnp.int32).at[dest].set((order // MOE_TOPK).astype(jnp.int32))
    tile_start = jnp.arange(n_tiles) * tm
    tile_e = jnp.minimum(jnp.sum(pad_end[None, :] <= tile_start[:, None], axis=1), MOE_EXPERTS - 1).astype(jnp.int32)
    tile_valid = (tile_start < pad_end[-1]).astype(jnp.int32)
    dest_by_assign = jnp.zeros((a,), jnp.int32).at[order].set(dest.astype(jnp.int32)).reshape(t, MOE_TOPK)
    return src, tile_e, tile_valid, dest_by_assign


def _rel_bucket(dist):
    n = jnp.maximum(dist, 0)
    exact = REL_BUCKETS // 2
    logv = jnp.log(jnp.maximum(n, 1).astype(F32) / exact) / math.log(REL_MAX_DIST / exact)
    large = jnp.minimum(exact + (logv * (REL_BUCKETS - exact)).astype(jnp.int32), REL_BUCKETS - 1)
    return jnp.where(n < exact, n, large)


def _bias_heads(rel_bias, dist, ok, shift=None):
    b = jnp.transpose(rel_bias[_rel_bucket(dist)].astype(F32), (2, 0, 1))
    if shift is not None:
        b = b - shift[:, None, None]
    return jnp.where(ok[None], b, NEG_INF)


def _bias_rows(rel_bias, dist, ok, q_len):
    return _bias_heads(rel_bias, dist, ok).reshape(N_HEADS * q_len, dist.shape[1])


def _band_tables(rel_bias, n_delta, window, shift=None):
    qi = jnp.arange(Q_BLOCK)[:, None]
    ki = jnp.arange(Q_BLOCK)[None, :]
    tabs = []
    for delta in range(n_delta):
        d = delta * Q_BLOCK + qi - ki
        tabs.append(_bias_heads(rel_bias, d, (d >= 0) & (d <= window), shift))
    return jnp.stack(tabs)


def _build_qbd(q, n_groups):
    r = q.shape[0]
    hpg = N_HEADS // n_groups
    blocks = []
    for hh in range(N_HEADS):
        g = hh // hpg
        parts = []
        if g > 0:
            parts.append(jnp.zeros((r, g * HEAD_DIM), F32))
        parts.append(q[:, hh * HEAD_DIM:(hh + 1) * HEAD_DIM])
        if g < n_groups - 1:
            parts.append(jnp.zeros((r, (n_groups - 1 - g) * HEAD_DIM), F32))
        blocks.append(jnp.concatenate(parts, axis=1))
    return (jnp.concatenate(blocks, axis=0) * (HEAD_DIM ** -0.5)).astype(BF16)


def _aug_values(v_t):
    lane = lax.broadcasted_iota(jnp.int32, v_t.shape, 1)
    one = jnp.ones(v_t.shape, v_t.dtype)
    return [jnp.where(lane < HEAD_DIM, v_t, one), jnp.where(lane >= HEAD_DIM, v_t, one)]


def _flash_update(s, v_t, m_ref, acc_ref):
    m_prev = m_ref[...]
    m_new = jnp.maximum(m_prev, jnp.max(s, axis=1, keepdims=True))
    alpha = jnp.exp(m_prev - m_new)
    p = jnp.exp(s - m_new).astype(BF16)
    v_aug = _aug_values(v_t)
    half = p.shape[0] // 2
    pv = jnp.concatenate([jnp.dot(p[:half], v_aug[0], preferred_element_type=F32),
                          jnp.dot(p[half:], v_aug[1], preferred_element_type=F32)], axis=0)
    acc_ref[...] = alpha * acc_ref[...] + pv
    m_ref[...] = m_new


def _flash_result(acc_h):
    return acc_h / pltpu.roll(acc_h, HEAD_DIM, axis=1)


def _rank_select(score, cand, b_idx, n_blocks, k):
    rank = jnp.zeros(score.shape, F32)
    for bp in range(n_blocks):
        col = score[..., bp:bp + 1]
        tie = jnp.where(b_idx > bp, 1.0, 0.0)
        rank = rank + jnp.where(col > score, 1.0, jnp.where(col == score, tie, 0.0))
    return jnp.where(cand, jnp.where(rank < float(k), 1.0, 0.0), 0.0)


def _heads_to_lanes(read_rows, n_groups, r):
    hpg = N_HEADS // n_groups
    outs = []
    for hh in range(N_HEADS):
        g = hh // hpg
        outs.append(read_rows(hh)[:, g * HEAD_DIM:(g + 1) * HEAD_DIM])
    return jnp.concatenate(outs, axis=1)


def _nsa_prompt_kernel(q_ref, kcs_ref, vcs_ref, ks_ref, vs_ref, kw_ref, vw_ref, gl_ref, bg_ref,
                       tab_ref, bc_ref, ex_ref, o_ref,
                       m_ref, acc_ref, oc_ref, os_ref, sel_ref, *, nb):
    j = pl.program_id(1)
    g_n = NSA_KV_HEADS
    hpg = N_HEADS // g_n
    rows = N_HEADS * Q_BLOCK
    qbd = _build_qbd(q_ref[...].astype(F32), g_n)

    s_c = _dot_nt(qbd, kcs_ref[...].astype(BF16)) + bc_ref[0]
    qi = lax.broadcasted_iota(jnp.int32, (Q_BLOCK, nb), 0)
    bi = lax.broadcasted_iota(jnp.int32, (Q_BLOCK, nb), 1)
    t = j * Q_BLOCK + qi
    valid_c = (t - (bi + 1) * NSA_CMP_BLOCK + 1) >= 0
    e_c = jnp.exp(s_c - jnp.max(s_c, axis=1, keepdims=True))
    p_c = e_c / jnp.sum(e_c, axis=1, keepdims=True)
    p_c4 = jnp.where(valid_c[None, None], p_c.reshape(g_n, hpg, Q_BLOCK, nb), 0.0)
    oc_ref[...] = jnp.dot(p_c4.reshape(rows, nb).astype(BF16), vcs_ref[...].astype(BF16),
                          preferred_element_type=F32)

    imp = jnp.sum(p_c4, axis=1)
    cur = t // NSA_CMP_BLOCK
    forced = (bi == cur) | (bi == cur - 1) | (bi == 0)
    cand = bi <= cur
    for g in range(g_n):
        score = jnp.where(cand, jnp.where(forced, NSA_FORCED_SCORE, imp[g]), -1.0)
        sel_ref[g] = _rank_select(score, cand, bi, nb, min(NSA_TOPK, nb)).astype(BF16)

    def reset():
        m_ref[...] = jnp.full(m_ref.shape, NEG_INF, F32)
        acc_ref[...] = jnp.zeros(acc_ref.shape, F32)

    def masked(s, kt):
        e_t = ex_ref[kt]
        msk = jnp.stack([jnp.dot(sel_ref[g], e_t, preferred_element_type=F32) for g in range(g_n)])
        madd = (1.0 - msk) * NEG_INF
        return (s.reshape(g_n, hpg, Q_BLOCK, Q_BLOCK) + madd[:, None]).reshape(rows, Q_BLOCK)

    reset()

    def sel_far(kt, carry):
        off = pl.multiple_of(kt * Q_BLOCK, Q_BLOCK)
        s = masked(_dot_nt(qbd, ks_ref[pl.ds(off, Q_BLOCK), :]), kt)
        _flash_update(s, vs_ref[pl.ds(off, Q_BLOCK), :], m_ref, acc_ref)
        return carry

    n_far = jnp.maximum(j - 1, 0)
    lax.fori_loop(0, n_far, sel_far, 0)

    def sel_near(kt, carry):
        off = pl.multiple_of(kt * Q_BLOCK, Q_BLOCK)
        s = masked(_dot_nt(qbd, ks_ref[pl.ds(off, Q_BLOCK), :]) + tab_ref[j - kt], kt)
        _flash_update(s, vs_ref[pl.ds(off, Q_BLOCK), :], m_ref, acc_ref)
        return carry

    lax.fori_loop(n_far, j + 1, sel_near, 0)
    os_ref[...] = acc_ref[...]

    reset()

    def win_tile(i, carry):
        off = pl.multiple_of((j - i) * Q_BLOCK, Q_BLOCK)
        s = _dot_nt(qbd, kw_ref[pl.ds(off, Q_BLOCK), :]) + tab_ref[i]
        _flash_update(s, vw_ref[pl.ds(off, Q_BLOCK), :], m_ref, acc_ref)
        return carry

    lax.fori_loop(0, jnp.minimum(j, NSA_WINDOW // Q_BLOCK) + 1, win_tile, 0)

    sig = jax.nn.sigmoid(gl_ref[...] + bg_ref[...])

    def head_out(hh):
        r0 = hh * Q_BLOCK
        return (sig[:, 3 * hh:3 * hh + 1] * oc_ref[r0:r0 + Q_BLOCK, :]
                + sig[:, 3 * hh + 1:3 * hh + 2] * _flash_result(os_ref[r0:r0 + Q_BLOCK, :])
                + sig[:, 3 * hh + 2:3 * hh + 3] * _flash_result(acc_ref[r0:r0 + Q_BLOCK, :]))

    o_ref[...] = _heads_to_lanes(head_out, g_n, Q_BLOCK).astype(o_ref.dtype)


def _nsa_prompt_attn(y, yb, kcs, vcs, b_gate_row, tabs, bc, ex, n_batch, seq):
    nq = seq // Q_BLOCK
    nb = seq // NSA_CMP_BLOCK
    rows = N_HEADS * Q_BLOCK
    c = NSA_COLS
    kv = lambda col: pl.BlockSpec((seq, LANES), lambda n, j: (n, col))
    const = lambda shape: pl.BlockSpec(shape, lambda n, j: (0,) * len(shape))
    stacked = pltpu.VMEM((rows, LANES), F32)
    return pl.pallas_call(
        functools.partial(_nsa_prompt_kernel, nb=nb),
        grid=(n_batch, nq),
        in_specs=[
            pl.BlockSpec((Q_BLOCK, D_MODEL), lambda n, j: (n * nq + j, 0)),
            pl.BlockSpec((nb, LANES), lambda n, j: (n, 0)),
            pl.BlockSpec((nb, LANES), lambda n, j: (n, 0)),
            kv(c["ks"]), kv(c["vs"]), kv(c["kw"]), kv(c["vw"]),
            pl.BlockSpec((Q_BLOCK, LANES), lambda n, j: (n * nq + j, c["gate"])),
            const((1, LANES)),
            const(tabs.shape),
            pl.BlockSpec((1, rows, nb), lambda n, j: (j, 0, 0)),
            const(ex.shape),
        ],
        out_specs=pl.BlockSpec((Q_BLOCK, D_MODEL), lambda n, j: (n * nq + j, 0)),
        out_shape=jax.ShapeDtypeStruct((n_batch * seq, D_MODEL), BF16),
        scratch_shapes=[stacked, stacked, stacked, stacked, pltpu.VMEM((NSA_KV_HEADS, Q_BLOCK, nb), BF16)],
        compiler_params=_cparams("parallel", "arbitrary"),
        name="nsa_prompt",
    )(yb, kcs, vcs, yb, yb, yb, yb, y, b_gate_row, tabs, bc, ex)


def _suffix_sums(u, tri):
    u_hi = u.astype(BF16)
    u_lo = (u - u_hi.astype(F32)).astype(BF16)
    return (jnp.dot(u_hi, tri, preferred_element_type=F32) + jnp.dot(u_lo, tri, preferred_element_type=F32))


def _stick_break_tile(z, v_t, tri, c_prev, valid):
    sp = jnp.log1p(jnp.exp(-jnp.abs(z)))
    ls = jnp.minimum(z, 0.0) - sp
    u = -jnp.maximum(z, 0.0) - sp
    if valid is not None:
        u = jnp.where(valid, u, 0.0)
    a = jnp.exp(ls + _suffix_sums(u, tri) + c_prev)
    if valid is not None:
        a = jnp.where(valid, a, 0.0)
    return (jnp.dot(a.astype(BF16), v_t, preferred_element_type=F32),
            c_prev + jnp.sum(u, axis=1, keepdims=True))


def _sb_prompt_kernel(q_ref, k_ref, v_ref, tri_ref, o_ref, acc_ref, c_ref):
    j = pl.program_id(1)
    g_n = SB_KV_HEADS
    rows = N_HEADS * Q_BLOCK
    qbd = _build_qbd(q_ref[...].astype(F32), g_n)
    acc_ref[...] = jnp.zeros(acc_ref.shape, F32)
    c_ref[...] = jnp.zeros(c_ref.shape, F32)

    def tile(off, valid):
        z = _dot_nt(qbd, k_ref[pl.ds(off, Q_BLOCK), :])
        av, c_new = _stick_break_tile(z, v_ref[pl.ds(off, Q_BLOCK), :], tri_ref[...], c_ref[...], valid)
        acc_ref[...] += av
        c_ref[...] = c_new

    qi = lax.broadcasted_iota(jnp.int32, (rows, Q_BLOCK), 0) % Q_BLOCK
    ki = lax.broadcasted_iota(jnp.int32, (rows, Q_BLOCK), 1)
    tile(pl.multiple_of(j * Q_BLOCK, Q_BLOCK), ki < qi)

    def body(i, carry):
        tile(pl.multiple_of((j - i) * Q_BLOCK, Q_BLOCK), None)
        return carry

    lax.fori_loop(1, j + 1, body, 0)
    o_ref[...] = _heads_to_lanes(lambda hh: acc_ref[hh * Q_BLOCK:(hh + 1) * Q_BLOCK, :], g_n,
                                 Q_BLOCK).astype(o_ref.dtype)


def _tri_strict_lower():
    qi = jnp.arange(Q_BLOCK)[:, None]
    ki = jnp.arange(Q_BLOCK)[None, :]
    return (qi > ki).astype(BF16)


def _sb_prompt_attn(yb, n_batch, seq):
    nq = seq // Q_BLOCK
    kvw = SB_KV_HEADS * HEAD_DIM
    rows = N_HEADS * Q_BLOCK
    return pl.pallas_call(
        _sb_prompt_kernel,
        grid=(n_batch, nq),
        in_specs=[
            pl.BlockSpec((Q_BLOCK, D_MODEL), lambda n, j: (n * nq + j, 0)),
            pl.BlockSpec((seq, kvw), lambda n, j: (n, D_MODEL // kvw)),
            pl.BlockSpec((seq, kvw), lambda n, j: (n, D_MODEL // kvw + 1)),
            pl.BlockSpec((Q_BLOCK, Q_BLOCK), lambda n, j: (0, 0)),
        ],
        out_specs=pl.BlockSpec((Q_BLOCK, D_MODEL), lambda n, j: (n * nq + j, 0)),
        out_shape=jax.ShapeDtypeStruct((n_batch * seq, D_MODEL), BF16),
        scratch_shapes=[pltpu.VMEM((rows, kvw), F32), pltpu.VMEM((rows, Q_BLOCK), F32)],
        compiler_params=_cparams("parallel", "arbitrary"),
        name="sb_prompt",
    )(yb, yb, yb, _tri_strict_lower())


def _swa_prompt_kernel(q_ref, k_ref, v_ref, tab_ref, sink_ref, o_ref, m_ref, acc_ref):
    j = pl.program_id(1)
    g_n = SWA_KV_HEADS
    rows = N_HEADS * Q_BLOCK
    qbd = _build_qbd(q_ref[...].astype(F32), g_n)
    m_ref[...] = sink_ref[...]
    lane = lax.broadcasted_iota(jnp.int32, (rows, LANES), 1)
    row = lax.broadcasted_iota(jnp.int32, (rows, LANES), 0)
    own = (lane >= HEAD_DIM) == (row >= rows // g_n)
    acc_ref[...] = jnp.where(own, 0.0, 1.0)

    def body(i, carry):
        off = pl.multiple_of((j - i) * Q_BLOCK, Q_BLOCK)
        s = _dot_nt(qbd, k_ref[pl.ds(off, Q_BLOCK), :]) + tab_ref[i]
        _flash_update(s, v_ref[pl.ds(off, Q_BLOCK), :], m_ref, acc_ref)
        return carry

    lax.fori_loop(0, jnp.minimum(j, SWA_WINDOW // Q_BLOCK) + 1, body, 0)
    o_ref[...] = _heads_to_lanes(lambda hh: _flash_result(acc_ref[hh * Q_BLOCK:(hh + 1) * Q_BLOCK, :]), g_n,
                                 Q_BLOCK).astype(o_ref.dtype)


def _swa_prompt_attn(yb, tabs, sink_rows, n_batch, seq):
    nq = seq // Q_BLOCK
    rows = N_HEADS * Q_BLOCK
    c = SWA_COLS
    return pl.pallas_call(
        _swa_prompt_kernel,
        grid=(n_batch, nq),
        in_specs=[
            pl.BlockSpec((Q_BLOCK, D_MODEL), lambda n, j: (n * nq + j, 0)),
            pl.BlockSpec((seq, LANES), lambda n, j: (n, c["k"])),
            pl.BlockSpec((seq, LANES), lambda n, j: (n, c["v"])),
            pl.BlockSpec(tabs.shape, lambda n, j: (0, 0, 0)),
            pl.BlockSpec((rows, LANES), lambda n, j: (0, 0)),
        ],
        out_specs=pl.BlockSpec((Q_BLOCK, D_MODEL), lambda n, j: (n * nq + j, 0)),
        out_shape=jax.ShapeDtypeStruct((n_batch * seq, D_MODEL), BF16),
        scratch_shapes=[pltpu.VMEM((rows, LANES), F32), pltpu.VMEM((rows, LANES), F32)],
        compiler_params=_cparams("parallel", "arbitrary"),
        name="swa_prompt",
    )(yb, yb, yb, tabs, sink_rows)


def _pages_t(pool):
    l, n_pool, page, g, hd = pool.shape
    return jnp.transpose(pool, (0, 1, 3, 4, 2)).reshape(l * n_pool, g * hd, page)


def _buf_t(buf):
    n, w, g, hd = buf.shape
    return jnp.transpose(buf, (0, 2, 3, 1)).reshape(n, g * hd, w)


def _pad_rows(x, n):
    return jnp.concatenate([x, jnp.zeros((n - x.shape[0], x.shape[1]), x.dtype)], axis=0)


def _pool_compress_kernel(x_ref, w_ref, o_ref, *, n_pages, g_n):
    kvw = g_n * HEAD_DIM
    accs = [jnp.zeros((n_pages, LANES), F32) for _ in range(g_n)]
    for r in range(kvw):
        x_r = x_ref[pl.ds(r, n_pages, stride=kvw), :].astype(BF16)
        accs[r // HEAD_DIM] = accs[r // HEAD_DIM] + jnp.dot(x_r, w_ref[r], preferred_element_type=F32)
    o_ref[...] = jnp.concatenate(accs, axis=1)


def _pool_compress(pool_t2d, w_rows, slot, n_pool, g_n):
    kvw = g_n * HEAD_DIM
    n_pages = next(c for c in (POOL_PAGES_PER_STEP, 64, 32, 16, 8) if n_pool % c == 0)
    steps = n_pool // n_pages
    return pl.pallas_call(
        functools.partial(_pool_compress_kernel, n_pages=n_pages, g_n=g_n),
        grid=(steps,),
        in_specs=[
            pl.BlockSpec((n_pages * kvw, PAGE_SIZE), lambda i: (i + slot * steps, 0)),
            pl.BlockSpec(w_rows.shape, lambda i: (0, 0, 0)),
        ],
        out_specs=pl.BlockSpec((n_pages, g_n * LANES), lambda i: (i, 0)),
        out_shape=jax.ShapeDtypeStruct((n_pool, g_n * LANES), F32),
        compiler_params=_cparams("parallel"),
        name="pool_compress",
    )(pool_t2d, w_rows)


def _nsa_sample_kernel(pt_ref, q_ref, ksn_ref, vsn_ref, kwn_ref, vwn_ref, gl_ref, bg_ref,
                       kcs_ref, vcs_ref, bcs_ref, tlast_ref, tfar_ref, tnew_ref, wtb_ref, ex_ref,
                       kwb_ref, vwb_ref, *rest, ppc, nchunk, nb, t_new):
    k_pages = rest[:ppc]
    v_pages = rest[ppc:2 * ppc]
    o_ref = rest[2 * ppc]
    m_ref, l_ref, acc_ref, oc_ref, ow_ref, sel_ref = rest[2 * ppc + 1:]
    c = pl.program_id(1)
    g_n = NSA_KV_HEADS
    hpg = N_HEADS // g_n
    rows = N_HEADS * t_new
    nbp = kcs_ref.shape[0]
    qbd = _build_qbd(q_ref[...], g_n)

    @pl.when(c == 0)
    def _():
        s_c = _dot_nt(qbd, kcs_ref[...].astype(BF16)) + bcs_ref[...]
        bi = lax.broadcasted_iota(jnp.int32, (rows, nbp), 1)
        e_c = jnp.exp(s_c - jnp.max(s_c, axis=1, keepdims=True))
        p_c = jnp.where(bi < nb - 1, e_c / jnp.sum(e_c, axis=1, keepdims=True), 0.0)
        oc_ref[...] = jnp.dot(p_c.astype(BF16), vcs_ref[...].astype(BF16), preferred_element_type=F32)
        imp = jnp.sum(p_c.reshape(g_n, hpg, t_new, nbp), axis=1)
        b3 = lax.broadcasted_iota(jnp.int32, (g_n, t_new, nbp), 2)
        forced = (b3 == nb - 1) | (b3 == nb - 2) | (b3 == 0)
        cand = b3 <= nb - 1
        score = jnp.where(cand, jnp.where(forced, NSA_FORCED_SCORE, imp), -1.0)
        sel = _rank_select(score, cand, b3, nb, min(NSA_TOPK, nb))
        sel_ref[...] = jnp.broadcast_to(sel[:, None], (g_n, hpg, t_new, nbp)).reshape(rows, nbp).astype(BF16)

        s_b = jnp.dot(qbd, kwb_ref[...].astype(BF16), preferred_element_type=F32) + wtb_ref[...]
        s_n = _dot_nt(qbd, _pad_rows(kwn_ref[...], LANES).astype(BF16)) + tnew_ref[...]
        m_w = jnp.maximum(jnp.max(s_b, axis=1, keepdims=True), jnp.max(s_n, axis=1, keepdims=True))
        p_b = jnp.exp(s_b - m_w)
        p_n = jnp.exp(s_n - m_w)
        l_w = jnp.sum(p_b, axis=1, keepdims=True) + jnp.sum(p_n, axis=1, keepdims=True)
        o_w = (_dot_nt(p_b.astype(BF16), vwb_ref[...].astype(BF16))
               + jnp.dot(p_n.astype(BF16), _pad_rows(vwn_ref[...], LANES).astype(BF16),
                         preferred_element_type=F32))
        ow_ref[...] = o_w / l_w

        s0 = _dot_nt(qbd, _pad_rows(ksn_ref[...], LANES).astype(BF16)) + tnew_ref[...]
        m0 = jnp.max(s0, axis=1, keepdims=True)
        p0 = jnp.exp(s0 - m0)
        m_ref[...] = jnp.broadcast_to(m0, m_ref.shape)
        l_ref[...] = jnp.broadcast_to(jnp.sum(p0, axis=1, keepdims=True), l_ref.shape)
        acc_ref[...] = jnp.dot(p0.astype(BF16), _pad_rows(vsn_ref[...], LANES).astype(BF16),
                               preferred_element_type=F32)

    msk = jnp.dot(sel_ref[...], ex_ref[...], preferred_element_type=F32)
    parts = []
    for r in range(ppc):
        s_r = jnp.dot(qbd, k_pages[r][...].astype(BF16), preferred_element_type=F32)
        if r == ppc - 1:
            s_r = s_r + jnp.where(c == nchunk - 1, tlast_ref[...], tfar_ref[...])
        else:
            s_r = s_r + tfar_ref[...]
        parts.append(s_r)
    s = jnp.concatenate(parts, axis=1) + (1.0 - msk) * NEG_INF
    m_prev = m_ref[...]
    m_new = jnp.maximum(m_prev, jnp.max(s, axis=1, keepdims=True))
    alpha = jnp.exp(m_prev - m_new)
    p = jnp.exp(s - m_new[:, 0:1])
    pv = _dot_nt(p[:, 0:PAGE_SIZE].astype(BF16), v_pages[0][...].astype(BF16))
    for r in range(1, ppc):
        pv = pv + _dot_nt(p[:, r * PAGE_SIZE:(r + 1) * PAGE_SIZE].astype(BF16), v_pages[r][...].astype(BF16))
    l_ref[...] = alpha * l_ref[...] + jnp.sum(p, axis=1, keepdims=True)
    acc_ref[...] = alpha * acc_ref[...] + pv
    m_ref[...] = m_new

    @pl.when(c == nchunk - 1)
    def _():
        sig = jax.nn.sigmoid(gl_ref[...] + bg_ref[...])

        def head_out(hh):
            r0 = hh * t_new
            o_s = acc_ref[r0:r0 + t_new, :] / l_ref[r0:r0 + t_new, :]
            return (sig[:, 3 * hh:3 * hh + 1] * oc_ref[r0:r0 + t_new, :]
                    + sig[:, 3 * hh + 1:3 * hh + 2] * o_s
                    + sig[:, 3 * hh + 2:3 * hh + 3] * ow_ref[r0:r0 + t_new, :])

        o_ref[...] = _heads_to_lanes(head_out, g_n, t_new).astype(o_ref.dtype)


def _nsa_sample_tables(rel_bias, past, t_new, wb, nb, nbp):
    i = jnp.arange(t_new)[:, None]
    ki = jnp.arange(LANES)[None, :]
    d_new = i - ki
    tnew = _bias_rows(rel_bias, d_new, (d_new >= 0) & (ki < t_new), t_new)
    d_last = PAGE_SIZE + i - ki
    tlast = _bias_rows(rel_bias, d_last, d_last >= 0, t_new)
    d_far = jnp.full((t_new, LANES), 2 * PAGE_SIZE)
    tfar = _bias_rows(rel_bias, d_far, d_far >= 0, t_new)
    r = jnp.arange(wb)[None, :]
    d_w = wb + i - r
    wtb = _bias_rows(rel_bias, d_w, (d_w <= NSA_WINDOW) & (past - wb + r >= 0), t_new)
    b = jnp.arange(nbp)[None, :]
    d_c = past + i - ((b + 1) * NSA_CMP_BLOCK - 1)
    bcs = _bias_rows(rel_bias, d_c, (d_c >= 0) & (b < nb), t_new)
    return tnew, tlast, tfar, wtb, bcs


def _nsa_sample_attn(y, kcs, vcs, b_gate_row, tables, ex, kwb_t, vwb_t, pool_k_t, pool_v_t, pt_flat,
                     n_seq, t_new, n_pages, nb):
    ppc = PAGES_PER_STEP
    nchunk = n_pages // ppc
    rows = N_HEADS * t_new
    nbp = kcs.shape[1]
    wb = kwb_t.shape[2]
    tnew, tlast, tfar, wtb, bcs = tables
    c = NSA_COLS
    new = lambda col: pl.BlockSpec((t_new, LANES), lambda n, ch, pt: (n, col))
    const = lambda shape: pl.BlockSpec(shape, lambda n, ch, pt: (0,) * len(shape))

    def page_spec(r):
        return pl.BlockSpec((None, LANES, PAGE_SIZE),
                            lambda n, ch, pt: (pt[n * n_pages + ch * ppc + r], 0, 0))

    in_specs = [
        pl.BlockSpec((t_new, D_MODEL), lambda n, ch, pt: (n, 0)),
        new(c["ks"]), new(c["vs"]), new(c["kw"]), new(c["vw"]), new(c["gate"]),
        const((1, LANES)),
        pl.BlockSpec((None, nbp, LANES), lambda n, ch, pt: (n, 0, 0)),
        pl.BlockSpec((None, nbp, LANES), lambda n, ch, pt: (n, 0, 0)),
        const((rows, nbp)), const((rows, LANES)), const((rows, LANES)), const((rows, LANES)),
        const((rows, wb)),
        pl.BlockSpec((nbp, ppc * PAGE_SIZE), lambda n, ch, pt: (0, ch)),
        pl.BlockSpec((None, LANES, wb), lambda n, ch, pt: (n, 0, 0)),
        pl.BlockSpec((None, LANES, wb), lambda n, ch, pt: (n, 0, 0)),
    ] + [page_spec(r) for r in range(ppc)] + [page_spec(r) for r in range(ppc)]
    grid_spec = pltpu.PrefetchScalarGridSpec(
        num_scalar_prefetch=1,
        grid=(n_seq, nchunk),
        in_specs=in_specs,
        out_specs=pl.BlockSpec((t_new, D_MODEL), lambda n, ch, pt: (n, 0)),
        scratch_shapes=[pltpu.VMEM((rows, LANES), F32)] * 5 + [pltpu.VMEM((rows, nbp), BF16)],
    )
    return pl.pallas_call(
        functools.partial(_nsa_sample_kernel, ppc=ppc, nchunk=nchunk, nb=nb, t_new=t_new),
        grid_spec=grid_spec,
        out_shape=jax.ShapeDtypeStruct((n_seq * t_new, D_MODEL), BF16),
        compiler_params=_cparams("parallel", "arbitrary"),
        name="nsa_sample",
    )(pt_flat, y, y, y, y, y, y, b_gate_row, kcs, vcs, bcs, tlast, tfar, tnew, wtb, ex, kwb_t, vwb_t,
      *([pool_k_t] * ppc), *([pool_v_t] * ppc))


def _sb_sample_kernel(pt_ref, q_ref, kn_ref, vn_ref, tri_ref, *rest, ppc, t_new):
    k_pages = rest[:ppc]
    v_pages = rest[ppc:2 * ppc]
    o_ref = rest[2 * ppc]
    acc_ref, c_ref = rest[2 * ppc + 1:]
    c = pl.program_id(1)
    g_n = SB_KV_HEADS
    rows = N_HEADS * t_new
    qbd = _build_qbd(q_ref[...], g_n)

    @pl.when(c == 0)
    def _():
        ki = lax.broadcasted_iota(jnp.int32, (rows, PAGE_SIZE), 1)
        qi = lax.broadcasted_iota(jnp.int32, (rows, PAGE_SIZE), 0) % t_new
        z = _dot_nt(qbd, _pad_rows(kn_ref[...], PAGE_SIZE).astype(BF16))
        av, c_new = _stick_break_tile(z, _pad_rows(vn_ref[...], PAGE_SIZE).astype(BF16), tri_ref[...],
                                      jnp.zeros((rows, PAGE_SIZE), F32), ki < qi)
        acc_ref[...] = av
        c_ref[...] = c_new

    for r in reversed(range(ppc)):
        z = jnp.dot(qbd, k_pages[r][...].astype(BF16), preferred_element_type=F32)
        sp = jnp.log1p(jnp.exp(-jnp.abs(z)))
        ls = jnp.minimum(z, 0.0) - sp
        u = -jnp.maximum(z, 0.0) - sp
        a = jnp.exp(ls + _suffix_sums(u, tri_ref[...]) + c_ref[...])
        acc_ref[...] += _dot_nt(a.astype(BF16), v_pages[r][...].astype(BF16))
        c_ref[...] += jnp.sum(u, axis=1, keepdims=True)

    @pl.when(c == pl.num_programs(1) - 1)
    def _():
        o_ref[...] = _heads_to_lanes(lambda hh: acc_ref[hh * t_new:(hh + 1) * t_new, :], g_n,
                                     t_new).astype(o_ref.dtype)


def _sb_sample_attn(y, pool_k_t, pool_v_t, pt_flat, n_seq, t_new, n_pages):
    ppc = PAGES_PER_STEP
    nchunk = n_pages // ppc
    rows = N_HEADS * t_new
    kvw = SB_KV_HEADS * HEAD_DIM

    def page_spec(r):
        return pl.BlockSpec((None, kvw, PAGE_SIZE),
                            lambda n, ch, pt: (pt[n * n_pages + (nchunk - 1 - ch) * ppc + r], 0, 0))

    in_specs = [
        pl.BlockSpec((t_new, D_MODEL), lambda n, ch, pt: (n, 0)),
        pl.BlockSpec((t_new, kvw), lambda n, ch, pt: (n, D_MODEL // kvw)),
        pl.BlockSpec((t_new, kvw), lambda n, ch, pt: (n, D_MODEL // kvw + 1)),
        pl.BlockSpec((PAGE_SIZE, PAGE_SIZE), lambda n, ch, pt: (0, 0)),
    ] + [page_spec(r) for r in range(ppc)] + [page_spec(r) for r in range(ppc)]
    grid_spec = pltpu.PrefetchScalarGridSpec(
        num_scalar_prefetch=1,
        grid=(n_seq, nchunk),
        in_specs=in_specs,
        out_specs=pl.BlockSpec((t_new, D_MODEL), lambda n, ch, pt: (n, 0)),
        scratch_shapes=[pltpu.VMEM((rows, kvw), F32), pltpu.VMEM((rows, PAGE_SIZE), F32)],
    )
    return pl.pallas_call(
        functools.partial(_sb_sample_kernel, ppc=ppc, t_new=t_new),
        grid_spec=grid_spec,
        out_shape=jax.ShapeDtypeStruct((n_seq * t_new, D_MODEL), BF16),
        compiler_params=_cparams("parallel", "arbitrary"),
        name="sb_sample",
    )(pt_flat, y, y, y, _tri_strict_lower(), *([pool_k_t] * ppc), *([pool_v_t] * ppc))


def _swa_sample_kernel(q_ref, kn_ref, vn_ref, kb_ref, vb_ref, tb_ref, tn_ref, sink_ref, o_ref, *, t_new):
    qbd = _build_qbd(q_ref[...], SWA_KV_HEADS)
    s_b = jnp.dot(qbd, kb_ref[...].astype(BF16), preferred_element_type=F32) + tb_ref[...]
    s_n = _dot_nt(qbd, _pad_rows(kn_ref[...], LANES).astype(BF16)) + tn_ref[...]
    sink = sink_ref[...][:, 0:1]
    m = jnp.maximum(jnp.maximum(jnp.max(s_b, axis=1, keepdims=True), jnp.max(s_n, axis=1, keepdims=True)), sink)
    p_b = jnp.exp(s_b - m)
    p_n = jnp.exp(s_n - m)
    l = jnp.sum(p_b, axis=1, keepdims=True) + jnp.sum(p_n, axis=1, keepdims=True) + jnp.exp(sink - m)
    o = (_dot_nt(p_b.astype(BF16), vb_ref[...].astype(BF16))
         + jnp.dot(p_n.astype(BF16), _pad_rows(vn_ref[...], LANES).astype(BF16),
                   preferred_element_type=F32)) / l
    o_ref[...] = _heads_to_lanes(lambda hh: o[hh * t_new:(hh + 1) * t_new, :], SWA_KV_HEADS,
                                 t_new).astype(o_ref.dtype)


def _swa_sample_attn(y, kb_t, vb_t, rel_bias, sinks, past, n_seq, t_new):
    rows = N_HEADS * t_new
    wb = kb_t.shape[2]
    i = jnp.arange(t_new)[:, None]
    r = jnp.arange(wb)[None, :]
    d_b = wb + i - r
    tb = _bias_rows(rel_bias, d_b, (d_b <= SWA_WINDOW) & (past - wb + r >= 0), t_new)
    ki = jnp.arange(LANES)[None, :]
    d_n = i - ki
    tn = _bias_rows(rel_bias, d_n, (d_n >= 0) & (ki < t_new), t_new)
    sink_rows = jnp.broadcast_to(jnp.repeat(sinks.astype(F32), t_new)[:, None], (rows, LANES))
    c = SWA_COLS
    return pl.pallas_call(
        functools.partial(_swa_sample_kernel, t_new=t_new),
        grid=(n_seq,),
        in_specs=[
            pl.BlockSpec((t_new, D_MODEL), lambda n: (n, 0)),
            pl.BlockSpec((t_new, LANES), lambda n: (n, c["k"])),
            pl.BlockSpec((t_new, LANES), lambda n: (n, c["v"])),
            pl.BlockSpec((None, LANES, wb), lambda n: (n, 0, 0)),
            pl.BlockSpec((None, LANES, wb), lambda n: (n, 0, 0)),
            pl.BlockSpec((rows, wb), lambda n: (0, 0)),
            pl.BlockSpec((rows, LANES), lambda n: (0, 0)),
            pl.BlockSpec((rows, LANES), lambda n: (0, 0)),
        ],
        out_specs=pl.BlockSpec((t_new, D_MODEL), lambda n: (n, 0)),
        out_shape=jax.ShapeDtypeStruct((n_seq * t_new, D_MODEL), BF16),
        compiler_params=_cparams("parallel"),
        name="swa_sample",
    )(y, y, y, kb_t, vb_t, tb, tn, sink_rows)


def _pad_cols(w, n):
    return jnp.pad(w, ((0, 0), (0, n - w.shape[1])))


def _compress_weight(w_phi):
    g_n = w_phi.shape[0]
    wp = jnp.transpose(w_phi, (1, 0, 2, 3))
    return jnp.einsum("jgde,gh->jgdhe", wp, jnp.eye(g_n, dtype=w_phi.dtype)).reshape(
        NSA_CMP_BLOCK * g_n * HEAD_DIM, g_n * HEAD_DIM)


def _compress_weight_rows(w_phi):
    bpp = PAGE_SIZE // NSA_CMP_BLOCK
    g_n = w_phi.shape[0]
    w = jnp.einsum("gjde,bc->gdbjce", w_phi, jnp.eye(bpp, dtype=w_phi.dtype))
    return w.reshape(g_n * HEAD_DIM, PAGE_SIZE, bpp * HEAD_DIM).astype(BF16)


def _expand_blocks(n_blocks_padded, n_keys):
    b = jnp.arange(n_blocks_padded)[:, None]
    k = jnp.arange(n_keys)[None, :]
    return (b == k // NSA_CMP_BLOCK).astype(BF16)


def _nsa_layer(xp, xs, mod_p, mod_s, slot, n_batch, seq, n_seq, t_new, pools, bufs, page_table, params,
               rel_bias, band_tabs):
    w_in, b_gate, w_phi_k, w_phi_v = params
    g_n = NSA_KV_HEADS
    kvw = g_n * HEAD_DIM
    w = _pad_cols(w_in, NSA_NTOT).astype(BF16)
    bias = jnp.zeros((NSA_NTOT,), F32)
    yp, ypb = _modmm(xp, mod_p[0], mod_p[1], w, bias, seq, second="b")
    ys, _ = _modmm(xs, mod_s[0], mod_s[1], w, bias, t_new, second="b")
    b_gate_row = jnp.pad(b_gate.astype(F32), (0, LANES - b_gate.shape[0])).reshape(1, LANES)
    wk = _compress_weight(w_phi_k)
    wv = _compress_weight(w_phi_v)
    col = lambda y, name: y[:, NSA_COLS[name] * LANES:(NSA_COLS[name] + 1) * LANES]

    nq = seq // Q_BLOCK
    nb = seq // NSA_CMP_BLOCK
    kc_p, vc_p = col(yp, "kc"), col(yp, "vc")
    flat_k = NSA_CMP_BLOCK * kvw
    kcs_p = _mm(kc_p.reshape(n_batch * nb, flat_k), wk, tm=min(256, n_batch * nb), tn=kvw, tk=2048)
    vcs_p = _mm(vc_p.reshape(n_batch * nb, flat_k), wv, tm=min(256, n_batch * nb), tn=kvw, tk=2048)
    t = jnp.arange(seq)[:, None]
    c_end = (jnp.arange(nb)[None, :] + 1) * NSA_CMP_BLOCK - 1
    d_c = t - c_end
    bc = _bias_heads(rel_bias, d_c, d_c >= 0).reshape(N_HEADS, nq, Q_BLOCK, nb).transpose(1, 0, 2, 3)
    bc = bc.reshape(nq, N_HEADS * Q_BLOCK, nb)
    ex = _expand_blocks(nb, seq).reshape(nb, nq, Q_BLOCK).transpose(1, 0, 2)
    op = _nsa_prompt_attn(yp, ypb, kcs_p, vcs_p, b_gate_row, band_tabs, bc, ex, n_batch, seq)

    n_pages = page_table.shape[1]
    past = n_pages * PAGE_SIZE
    pool_kc, pool_vc, pool_ks, pool_vs = pools
    n_pool = pool_kc.shape[1]
    nb_s = past // NSA_CMP_BLOCK + 1
    nbp = (nb_s + 15) // 16 * 16
    bpp = PAGE_SIZE // NSA_CMP_BLOCK

    def summaries(pool, new_rows, w_phi, wmat):
        s_all = _pool_compress(_pages_t(pool).reshape(-1, PAGE_SIZE), _compress_weight_rows(w_phi), slot,
                               n_pool, g_n)
        past_sum = s_all[page_table].reshape(n_seq, n_pages, g_n, bpp, HEAD_DIM)
        past_sum = jnp.transpose(past_sum, (0, 1, 3, 2, 4)).reshape(n_seq, n_pages * bpp, kvw)
        new_flat = new_rows.reshape(n_seq, t_new * kvw)
        last = _mm(new_flat, wmat[:t_new * kvw], tm=n_seq, tn=kvw, tk=t_new * kvw)
        pad = jnp.zeros((n_seq, nbp - nb_s, kvw), F32)
        return jnp.concatenate([past_sum, last[:, None, :], pad], axis=1)

    kcs_s = summaries(pool_kc, col(ys, "kc"), w_phi_k, wk)
    vcs_s = summaries(pool_vc, col(ys, "vc"), w_phi_v, wv)
    buf_kw, buf_vw = bufs
    wb = buf_kw.shape[1]
    tables = _nsa_sample_tables(rel_bias, past, t_new, wb, nb_s, nbp)
    ex_s = _expand_blocks(nbp, past)
    pt_flat = (page_table + slot * n_pool).reshape(-1).astype(jnp.int32)
    os_ = _nsa_sample_attn(ys, kcs_s, vcs_s, b_gate_row, tables, ex_s, _buf_t(buf_kw), _buf_t(buf_vw),
                           _pages_t(pool_ks), _pages_t(pool_vs), pt_flat, n_seq, t_new, n_pages, nb_s)

    shp_p = (n_batch, seq, NSA_KV_HEADS, HEAD_DIM)
    shp_s = (n_seq, t_new, NSA_KV_HEADS, HEAD_DIM)
    wbp = min(NSA_WINDOW, seq)
    st_p, st_s = [], []
    for name in ("kc", "vc", "ks", "vs"):
        st_p.append(col(yp, name).reshape(shp_p))
        st_s.append(col(ys, name).reshape(shp_s))
    for name, buf in (("kw", buf_kw), ("vw", buf_vw)):
        st_p.append(col(yp, name).reshape(shp_p)[:, seq - wbp:])
        st_s.append(jnp.concatenate([buf, col(ys, name).reshape(shp_s)], axis=1)[:, t_new:])
    return op, os_, st_p, st_s


def _sb_layer(xp, xs, mod_p, mod_s, slot, n_batch, seq, n_seq, t_new, pools, page_table, w_in):
    kvw = SB_KV_HEADS * HEAD_DIM
    w = w_in.astype(BF16)
    bias = jnp.zeros((SB_NTOT,), F32)
    yp, ypb = _modmm(xp, mod_p[0], mod_p[1], w, bias, seq, second="b")
    ys, _ = _modmm(xs, mod_s[0], mod_s[1], w, bias, t_new, second="b")
    op = _sb_prompt_attn(ypb, n_batch, seq)
    pool_k, pool_v = pools
    n_pool = pool_k.shape[1]
    n_pages = page_table.shape[1]
    pt_flat = (page_table + slot * n_pool).reshape(-1).astype(jnp.int32)
    os_ = _sb_sample_attn(ys, _pages_t(pool_k), _pages_t(pool_v), pt_flat, n_seq, t_new, n_pages)
    shp_p = (n_batch, seq, SB_KV_HEADS, HEAD_DIM)
    shp_s = (n_seq, t_new, SB_KV_HEADS, HEAD_DIM)
    st_p = [yp[:, D_MODEL:D_MODEL + kvw].reshape(shp_p), yp[:, D_MODEL + kvw:].reshape(shp_p)]
    st_s = [ys[:, D_MODEL:D_MODEL + kvw].reshape(shp_s), ys[:, D_MODEL + kvw:].reshape(shp_s)]
    return op, os_, st_p, st_s


def _swa_layer(xp, xs, mod_p, mod_s, n_batch, seq, n_seq, t_new, bufs, past, w_in, sinks, rel_bias):
    w = w_in.astype(BF16)
    bias = jnp.zeros((SWA_NTOT,), F32)
    yp, ypb = _modmm(xp, mod_p[0], mod_p[1], w, bias, seq, second="b")
    ys, _ = _modmm(xs, mod_s[0], mod_s[1], w, bias, t_new, second="b")
    rows = N_HEADS * Q_BLOCK
    tabs = _band_tables(rel_bias, SWA_WINDOW // Q_BLOCK + 1, SWA_WINDOW).reshape(-1, rows, Q_BLOCK)
    sink_rows = jnp.broadcast_to(jnp.repeat(sinks.astype(F32), Q_BLOCK)[:, None], (rows, LANES))
    op = _swa_prompt_attn(ypb, tabs, sink_rows, n_batch, seq)
    buf_k, buf_v = bufs
    os_ = _swa_sample_attn(ys, _buf_t(buf_k), _buf_t(buf_v), rel_bias, sinks, past, n_seq, t_new)
    shp_p = (n_batch, seq, SWA_KV_HEADS, HEAD_DIM)
    shp_s = (n_seq, t_new, SWA_KV_HEADS, HEAD_DIM)
    wbp = min(SWA_WINDOW, seq)
    col = lambda y, name: y[:, SWA_COLS[name] * LANES:(SWA_COLS[name] + 1) * LANES]
    st_p = [col(yp, n).reshape(shp_p)[:, seq - wbp:] for n in ("k", "v")]
    st_s = [jnp.concatenate([b, col(ys, n).reshape(shp_s)], axis=1)[:, t_new:]
            for n, b in (("k", buf_k), ("v", buf_v))]
    return op, os_, st_p, st_s


def _moe_layer(xp, xs, mod_p, mod_s, seq, t_new, layer, w_rg, b_rg, w_re, b_re, w_gate, w_up, w_down,
               ln_g, ln_b):
    tp = xp.shape[0]
    w_r = _pad_cols(jnp.concatenate([w_rg, w_re], axis=1), LANES).astype(BF16)
    b_r = jnp.pad(jnp.concatenate([b_rg, b_re]).astype(F32), (0, LANES - MOE_GROUPS - MOE_EXPERTS))
    lg_p, h_p = _modmm(xp, mod_p[0], mod_p[1], w_r, b_r, seq, second="h")
    lg_s, h_s = _modmm(xs, mod_s[0], mod_s[1], w_r, b_r, t_new, second="h")
    e_idx, e_w = _route(jnp.concatenate([lg_p, lg_s], axis=0))
    src, tile_e, tile_valid, dest = _moe_dispatch(e_idx, MOE_TILE)
    h_all = jnp.concatenate([h_p, h_s, jnp.zeros((8, D_MODEL), F32)], axis=0)
    out = _moe_ffn(h_all[src], tile_e + layer * MOE_EXPERTS, tile_valid, w_gate, w_up, w_down)
    f0, f1 = out[dest[:, 0]], out[dest[:, 1]]
    xp2 = _combine_res_ln(f0[:tp], f1[:tp], e_w[:tp], xp, mod_p[2], ln_g, ln_b, seq)
    xs2 = _combine_res_ln(f0[tp:], f1[tp:], e_w[tp:], xs, mod_s[2], ln_g, ln_b, t_new)
    return xp2, xs2


def kernel(x_prompt, x_sample, c_prompt, c_sample, cache_nsa_k_cmp, cache_nsa_v_cmp, cache_nsa_k_slc,
           cache_nsa_v_slc, cache_nsa_k_win, cache_nsa_v_win, cache_sb_k, cache_sb_v, cache_swa_k_win,
           cache_swa_v_win, page_table, rel_bias, w_ada, b_ada, ln1_g, ln1_b, ln2_g, ln2_b, w_in_nsa,
           b_gate_nsa, w_phi_k_nsa, w_phi_v_nsa, w_o_nsa, w_in_sb, w_o_sb, w_in_swa, sinks_swa, w_o_swa,
           w_route_group, b_route_group, w_route_expert, b_route_expert, w_exp_gate, w_exp_up, w_exp_down):
    n_batch, seq, _ = x_prompt.shape
    n_seq, t_new, _ = x_sample.shape
    past = page_table.shape[1] * PAGE_SIZE
    xp = x_prompt.reshape(n_batch * seq, D_MODEL)
    xs = x_sample.reshape(n_seq * t_new, D_MODEL)
    n_c = n_batch + n_seq
    n_c_pad = (n_c + 7) // 8 * 8
    c_all = jnp.concatenate([c_prompt, c_sample, jnp.zeros((n_c_pad - n_c, D_MODEL), F32)], axis=0)
    far_bias = rel_bias[REL_BUCKETS - 1].astype(F32)
    nsa_tabs = _band_tables(rel_bias, NSA_WINDOW // Q_BLOCK + 1, NSA_WINDOW, shift=far_bias).reshape(
        -1, N_HEADS * Q_BLOCK, Q_BLOCK)
    ts_tiles = (n_seq * t_new) // ROW_TILE
    w_gate_all = w_exp_gate.reshape(DEPTH * MOE_EXPERTS, D_MODEL, MOE_D_FF)
    w_up_all = w_exp_up.reshape(DEPTH * MOE_EXPERTS, D_MODEL, MOE_D_FF)
    w_down_all = w_exp_down.reshape(DEPTH * MOE_EXPERTS, MOE_D_FF, D_MODEL)

    st = {k: ([], []) for k in ("a", "b", "c")}
    for layer in range(DEPTH):
        kind, slot = layer % N_MIXERS, layer // N_MIXERS
        mods = _mm(c_all, w_ada[layer], b_ada[layer], act="silu", tm=n_c_pad, tn=1536, tk=D_MODEL)
        mp = [mods[:n_batch, k * D_MODEL:(k + 1) * D_MODEL].reshape(n_batch, 1, D_MODEL) for k in range(6)]
        ms = [jnp.repeat(mods[n_batch:n_c, k * D_MODEL:(k + 1) * D_MODEL], t_new, axis=0)
              .reshape(ts_tiles, ROW_TILE, D_MODEL) for k in range(6)]
        if kind == 0:
            op, os_, stp, sts = _nsa_layer(
                xp, xs, mp[0:2], ms[0:2], slot, n_batch, seq, n_seq, t_new,
                (cache_nsa_k_cmp, cache_nsa_v_cmp, cache_nsa_k_slc, cache_nsa_v_slc),
                (cache_nsa_k_win[slot], cache_nsa_v_win[slot]), page_table,
                (w_in_nsa[slot], b_gate_nsa[slot], w_phi_k_nsa[slot], w_phi_v_nsa[slot]), rel_bias, nsa_tabs)
            w_o, key = w_o_nsa[slot], "a"
        elif kind == 1:
            op, os_, stp, sts = _sb_layer(xp, xs, mp[0:2], ms[0:2], slot, n_batch, seq, n_seq, t_new,
                                          (cache_sb_k, cache_sb_v), page_table, w_in_sb[slot])
            w_o, key = w_o_sb[slot], "b"
        else:
            op, os_, stp, sts = _swa_layer(xp, xs, mp[0:2], ms[0:2], n_batch, seq, n_seq, t_new,
                                           (cache_swa_k_win[slot], cache_swa_v_win[slot]), past,
                                           w_in_swa[slot], sinks_swa[slot], rel_bias)
            w_o, key = w_o_swa[slot], "c"
        st[key][0].append(stp)
        st[key][1].append(sts)
        w_o = w_o.astype(BF16)
        xp = _proj_res_ln(op, xp, mp[2], w_o, ln1_g[layer], ln1_b[layer], seq)
        xs = _proj_res_ln(os_, xs, ms[2], w_o, ln1_g[layer], ln1_b[layer], t_new)
        xp, xs = _moe_layer(xp, xs, mp[3:6], ms[3:6], seq, t_new, layer, w_route_group[layer],
                            b_route_group[layer], w_route_expert[layer], b_route_expert[layer],
                            w_gate_all, w_up_all, w_down_all, ln2_g[layer], ln2_b[layer])

    outs = [xp.reshape(n_batch, seq, D_MODEL), xs.reshape(n_seq, t_new, D_MODEL)]
    for key, n_state in (("a", 6), ("b", 2), ("c", 2)):
        for i in range(n_state):
            outs.append(jnp.stack([s[i] for s in st[key][0]]))
            outs.append(jnp.stack([s[i] for s in st[key][1]]))
    return tuple(outs)
```

```python
import functools
import math

import jax
import jax.numpy as jnp
from jax import lax
from jax.experimental import pallas as pl
from jax.experimental.pallas import tpu as pltpu

F32 = jnp.float32
BF16 = jnp.bfloat16

D_MODEL = 1024
DEPTH = 4
PAGE_SIZE = 128
N_HEADS = 16
HEAD_DIM = D_MODEL // N_HEADS
N_MIXERS = 3
Q_BLOCK = 128

NSA_KV_HEADS = 2
NSA_CMP_BLOCK = 64
NSA_TOPK = 16
NSA_WINDOW = 512
NSA_FORCED_SCORE = float(N_HEADS + 1)
SB_KV_HEADS = 4
SWA_KV_HEADS = 2
SWA_WINDOW = 128

REL_BUCKETS = 32
REL_MAX_DIST = 128

MOE_GROUPS = 4
MOE_EXPERTS_PER_GROUP = 8
MOE_EXPERTS = MOE_GROUPS * MOE_EXPERTS_PER_GROUP
MOE_TOPK = 2
MOE_D_FF = 512

DEEPNORM_ALPHA = (2 * DEPTH) ** 0.25
LN_EPS = 1e-5
NEG_INF = -1e30

VMEM_LIMIT_BYTES = 56 * 1024 * 1024
LANES = 128
ROW_TILE = 256
MOE_TILE = 256
PAGES_PER_STEP = 16
POOL_PAGES_PER_STEP = 128

NSA_COLS = dict(kc=8, vc=9, ks=10, vs=11, kw=12, vw=13, gate=14)
NSA_NTOT = 15 * LANES
SB_NTOT = D_MODEL + 2 * SB_KV_HEADS * HEAD_DIM
SWA_COLS = dict(k=8, v=9)
SWA_NTOT = 10 * LANES


def _cparams(*sem):
    return pltpu.CompilerParams(dimension_semantics=sem, vmem_limit_bytes=VMEM_LIMIT_BYTES)


def _dot_nt(a, b):
    return lax.dot_general(a, b, (((1,), (1,)), ((), ())), preferred_element_type=F32)


def _mm_kernel(x_ref, w_ref, b_ref, o_ref, acc_ref, *, act, nk):
    k = pl.program_id(2)

    @pl.when(k == 0)
    def _():
        acc_ref[...] = jnp.zeros_like(acc_ref)

    x = x_ref[...].astype(F32)
    if act == "silu":
        x = x * jax.nn.sigmoid(x)
    acc_ref[...] += jnp.dot(x.astype(BF16), w_ref[...].astype(BF16), preferred_element_type=F32)

    @pl.when(k == nk - 1)
    def _():
        o_ref[...] = (acc_ref[...] + b_ref[...]).astype(o_ref.dtype)


def _mm(x, w, b=None, *, act=None, tm, tn, tk):
    m = x.shape[0]
    kdim, n = w.shape
    if b is None:
        b = jnp.zeros((n,), F32)
    b = b.reshape(1, n).astype(F32)
    nk = kdim // tk
    return pl.pallas_call(
        functools.partial(_mm_kernel, act=act, nk=nk),
        grid=(m // tm, n // tn, nk),
        in_specs=[
            pl.BlockSpec((tm, tk), lambda i, j, k: (i, k)),
            pl.BlockSpec((tk, tn), lambda i, j, k: (k, j)),
            pl.BlockSpec((1, tn), lambda i, j, k: (0, j)),
        ],
        out_specs=pl.BlockSpec((tm, tn), lambda i, j, k: (i, j)),
        out_shape=jax.ShapeDtypeStruct((m, n), F32),
        scratch_shapes=[pltpu.VMEM((tm, tn), F32)],
        compiler_params=_cparams("parallel", "parallel", "arbitrary"),
        name="mm",
    )(x, w, b)


def _modmm_kernel(x_ref, sh_ref, sc_ref, w_ref, b_ref, o_ref, x_ref2, *, second):
    h = x_ref[...] * (1.0 + sc_ref[0]) + sh_ref[0]
    y = jnp.dot(h.astype(BF16), w_ref[...], preferred_element_type=F32) + b_ref[...]
    o_ref[...] = y
    if second == "h":
        x_ref2[...] = h
    else:
        x_ref2[...] = y.astype(BF16)


def _mod_spec(mod, tm, rows_per_mod):
    r = mod.shape[1]
    if r == 1:
        tiles = rows_per_mod // tm
        return pl.BlockSpec((1, 1, D_MODEL), lambda i: (i // tiles, 0, 0))
    return pl.BlockSpec((1, tm, D_MODEL), lambda i: (i, 0, 0))


def _modmm(x, shift, scale, w_bf16, bias, rows_per_mod, *, second, tm=ROW_TILE):
    t = x.shape[0]
    n = w_bf16.shape[1]
    out_shape = [jax.ShapeDtypeStruct((t, n), F32)]
    out_specs = [pl.BlockSpec((tm, n), lambda i: (i, 0))]
    if second == "h":
        out_shape.append(jax.ShapeDtypeStruct((t, D_MODEL), F32))
        out_specs.append(pl.BlockSpec((tm, D_MODEL), lambda i: (i, 0)))
    else:
        out_shape.append(jax.ShapeDtypeStruct((t, n), BF16))
        out_specs.append(pl.BlockSpec((tm, n), lambda i: (i, 0)))
    return pl.pallas_call(
        functools.partial(_modmm_kernel, second=second),
        grid=(t // tm,),
        in_specs=[
            pl.BlockSpec((tm, D_MODEL), lambda i: (i, 0)),
            _mod_spec(shift, tm, rows_per_mod),
            _mod_spec(scale, tm, rows_per_mod),
            pl.BlockSpec((D_MODEL, n), lambda i: (0, 0)),
            pl.BlockSpec((1, n), lambda i: (0, 0)),
        ],
        out_specs=out_specs,
        out_shape=out_shape,
        compiler_params=_cparams("parallel"),
        name="modmm",
    )(x, shift, scale, w_bf16, bias.reshape(1, n).astype(F32))


def _layer_norm_rows(r, g, b):
    mu = jnp.mean(r, axis=-1, keepdims=True)
    d = r - mu
    var = jnp.mean(d * d, axis=-1, keepdims=True)
    return d * lax.rsqrt(var + LN_EPS) * g + b


def _proj_res_ln_kernel(a_ref, x_ref, gt_ref, w_ref, g_ref, b_ref, o_ref):
    y = jnp.dot(a_ref[...], w_ref[...], preferred_element_type=F32)
    r = DEEPNORM_ALPHA * x_ref[...] + (1.0 + gt_ref[0]) * y
    o_ref[...] = _layer_norm_rows(r, g_ref[...], b_ref[...])


def _proj_res_ln(a_bf16, x, gate, w_bf16, ln_g, ln_b, rows_per_mod, *, tm=ROW_TILE):
    t = x.shape[0]
    return pl.pallas_call(
        _proj_res_ln_kernel,
        grid=(t // tm,),
        in_specs=[
            pl.BlockSpec((tm, D_MODEL), lambda i: (i, 0)),
            pl.BlockSpec((tm, D_MODEL), lambda i: (i, 0)),
            _mod_spec(gate, tm, rows_per_mod),
            pl.BlockSpec((D_MODEL, D_MODEL), lambda i: (0, 0)),
            pl.BlockSpec((1, D_MODEL), lambda i: (0, 0)),
            pl.BlockSpec((1, D_MODEL), lambda i: (0, 0)),
        ],
        out_specs=pl.BlockSpec((tm, D_MODEL), lambda i: (i, 0)),
        out_shape=jax.ShapeDtypeStruct((t, D_MODEL), F32),
        compiler_params=_cparams("parallel"),
        name="proj_res_ln",
    )(a_bf16, x, gate, w_bf16, ln_g.reshape(1, -1), ln_b.reshape(1, -1))


def _combine_res_ln_kernel(f0_ref, f1_ref, w_ref, x_ref, gt_ref, g_ref, b_ref, o_ref):
    w = w_ref[...]
    f = f0_ref[...] * w[:, 0:1] + f1_ref[...] * w[:, 1:2]
    r = DEEPNORM_ALPHA * x_ref[...] + (1.0 + gt_ref[0]) * f
    o_ref[...] = _layer_norm_rows(r, g_ref[...], b_ref[...])


def _combine_res_ln(f0, f1, w, x, gate, ln_g, ln_b, rows_per_mod, *, tm=ROW_TILE):
    t = x.shape[0]
    row = lambda i: (i, 0)
    return pl.pallas_call(
        _combine_res_ln_kernel,
        grid=(t // tm,),
        in_specs=[
            pl.BlockSpec((tm, D_MODEL), row),
            pl.BlockSpec((tm, D_MODEL), row),
            pl.BlockSpec((tm, MOE_TOPK), row),
            pl.BlockSpec((tm, D_MODEL), row),
            _mod_spec(gate, tm, rows_per_mod),
            pl.BlockSpec((1, D_MODEL), lambda i: (0, 0)),
            pl.BlockSpec((1, D_MODEL), lambda i: (0, 0)),
        ],
        out_specs=pl.BlockSpec((tm, D_MODEL), row),
        out_shape=jax.ShapeDtypeStruct((t, D_MODEL), F32),
        compiler_params=_cparams("parallel"),
        name="combine_res_ln",
    )(f0, f1, w, x, gate, ln_g.reshape(1, -1), ln_b.reshape(1, -1))


def _moe_ffn_kernel(te_ref, tv_ref, x_ref, wg_ref, wu_ref, wd_ref, o_ref, wg_b, wu_b, wd_b):
    i = pl.program_id(0)
    e = te_ref[i]
    prev = te_ref[jnp.maximum(i - 1, 0)]

    @pl.when((i == 0) | (e != prev))
    def _():
        wg_b[...] = wg_ref[0].astype(BF16)
        wu_b[...] = wu_ref[0].astype(BF16)
        wd_b[...] = wd_ref[0].astype(BF16)

    @pl.when(tv_ref[i] > 0)
    def _():
        x = x_ref[...].astype(BF16)
        g = jnp.dot(x, wg_b[...], preferred_element_type=F32)
        u = jnp.dot(x, wu_b[...], preferred_element_type=F32)
        hmid = (g * jax.nn.sigmoid(g)) * u
        o_ref[...] = jnp.dot(hmid.astype(BF16), wd_b[...], preferred_element_type=F32)

    @pl.when(tv_ref[i] == 0)
    def _():
        o_ref[...] = jnp.zeros_like(o_ref)


def _moe_ffn(x_sorted, tile_e, tile_valid, w_gate, w_up, w_down, *, tm=MOE_TILE):
    p = x_sorted.shape[0]
    n_tiles = p // tm
    grid_spec = pltpu.PrefetchScalarGridSpec(
        num_scalar_prefetch=2,
        grid=(n_tiles,),
        in_specs=[
            pl.BlockSpec((tm, D_MODEL), lambda i, te, tv: (i, 0)),
            pl.BlockSpec((1, D_MODEL, MOE_D_FF), lambda i, te, tv: (te[i], 0, 0)),
            pl.BlockSpec((1, D_MODEL, MOE_D_FF), lambda i, te, tv: (te[i], 0, 0)),
            pl.BlockSpec((1, MOE_D_FF, D_MODEL), lambda i, te, tv: (te[i], 0, 0)),
        ],
        out_specs=pl.BlockSpec((tm, D_MODEL), lambda i, te, tv: (i, 0)),
        scratch_shapes=[
            pltpu.VMEM((D_MODEL, MOE_D_FF), BF16),
            pltpu.VMEM((D_MODEL, MOE_D_FF), BF16),
            pltpu.VMEM((MOE_D_FF, D_MODEL), BF16),
        ],
    )
    return pl.pallas_call(
        _moe_ffn_kernel,
        grid_spec=grid_spec,
        out_shape=jax.ShapeDtypeStruct((p, D_MODEL), F32),
        compiler_params=_cparams("arbitrary"),
        name="moe_ffn",
    )(tile_e, tile_valid, x_sorted, w_gate, w_up, w_down)


def _route(logits):
    g_logits = logits[:, :MOE_GROUPS]
    g_sel = jnp.argmax(g_logits, axis=-1)
    g_w = jnp.take_along_axis(jax.nn.softmax(g_logits, axis=-1), g_sel[:, None], axis=-1)
    e_logits = logits[:, MOE_GROUPS:MOE_GROUPS + MOE_EXPERTS].reshape(-1, MOE_GROUPS, MOE_EXPERTS_PER_GROUP)
    e_in = jnp.take_along_axis(e_logits, g_sel[:, None, None], axis=1)[:, 0]
    top_v, top_i = lax.top_k(e_in, MOE_TOPK)
    e_w = jax.nn.softmax(top_v, axis=-1) * g_w
    e_idx = (g_sel[:, None] * MOE_EXPERTS_PER_GROUP + top_i).astype(jnp.int32)
    return e_idx, e_w


def _moe_dispatch(e_idx, tm):
    t = e_idx.shape[0]
    a = t * MOE_TOPK
    flat_e = e_idx.reshape(-1)
    order = jnp.argsort(flat_e)
    sorted_e = flat_e[order]
    counts = jnp.bincount(flat_e, length=MOE_EXPERTS)
    padded = (counts + tm - 1) // tm * tm
    pad_end = jnp.cumsum(padded)
    dest = (pad_end - padded)[sorted_e] + jnp.arange(a) - (jnp.cumsum(counts) - counts)[sorted_e]
    n_tiles = (a + MOE_EXPERTS * (tm - 1)) // tm + 1
    src = jnp.full((n_tiles * tm,), t, jnp.int32).at[dest].set((order // MOE_TOPK).astype(jnp.int32))
    tile_start = jnp.arange(n_tiles) * tm
    tile_e = jnp.minimum(jnp.sum(pad_end[None, :] <= tile_start[:, None], axis=1), MOE_EXPERTS - 1).astype(jnp.int32)
    tile_valid = (tile_start < pad_end[-1]).astype(jnp.int32)
    dest_by_assign = jnp.zeros((a,), jnp.int32).at[order].set(dest.astype(jnp.int32)).reshape(t, MOE_TOPK)
    return src, tile_e, tile_valid, dest_by_assign


def _rel_bucket(dist):
    n = jnp.maximum(dist, 0)
    exact = REL_BUCKETS // 2
    logv = jnp.log(jnp.maximum(n, 1).astype(F32) / exact) / math.log(REL_MAX_DIST / exact)
    large = jnp.minimum(exact + (logv * (REL_BUCKETS - exact)).astype(jnp.int32), REL_BUCKETS - 1)
    return jnp.where(n < exact, n, large)


def _bias_heads(rel_bias, dist, ok, shift=None):
    onehot = jax.nn.one_hot(_rel_bucket(dist), REL_BUCKETS, dtype=F32)
    b = jnp.transpose(jnp.dot(onehot, rel_bias.astype(F32), precision=lax.Precision.HIGHEST), (2, 0, 1))
    if shift is not None:
        b = b - shift[:, None, None]
    return jnp.where(ok[None], b, NEG_INF)


def _bias_rows(rel_bias, dist, ok, q_len):
    return _bias_heads(rel_bias, dist, ok).reshape(N_HEADS * q_len, dist.shape[1])


def _band_tables(rel_bias, n_delta, window, shift=None):
    qi = jnp.arange(Q_BLOCK)[:, None]
    ki = jnp.arange(Q_BLOCK)[None, :]
    tabs = []
    for delta in range(n_delta):
        d = delta * Q_BLOCK + qi - ki
        tabs.append(_bias_heads(rel_bias, d, (d >= 0) & (d <= window), shift))
    return jnp.stack(tabs)


def _build_qbd(q, n_groups):
    r = q.shape[0]
    hpg = N_HEADS // n_groups
    blocks = []
    for hh in range(N_HEADS):
        g = hh // hpg
        parts = []
        if g > 0:
            parts.append(jnp.zeros((r, g * HEAD_DIM), F32))
        parts.append(q[:, hh * HEAD_DIM:(hh + 1) * HEAD_DIM])
        if g < n_groups - 1:
            parts.append(jnp.zeros((r, (n_groups - 1 - g) * HEAD_DIM), F32))
        blocks.append(jnp.concatenate(parts, axis=1))
    return (jnp.concatenate(blocks, axis=0) * (HEAD_DIM ** -0.5)).astype(BF16)


def _aug_values(v_t):
    lane = lax.broadcasted_iota(jnp.int32, v_t.shape, 1)
    one = jnp.ones(v_t.shape, v_t.dtype)
    return [jnp.where(lane < HEAD_DIM, v_t, one), jnp.where(lane >= HEAD_DIM, v_t, one)]


def _flash_update(s, v_t, m_ref, acc_ref):
    m_prev = m_ref[...]
    m_new = jnp.maximum(m_prev, jnp.max(s, axis=1, keepdims=True))
    alpha = jnp.exp(m_prev - m_new)
    n_rep = s.shape[1] // LANES
    p = jnp.exp(s - (m_new if n_rep == 1 else jnp.concatenate([m_new] * n_rep, axis=1))).astype(BF16)
    v_aug = _aug_values(v_t)
    half = p.shape[0] // 2
    pv = jnp.concatenate([jnp.dot(p[:half], v_aug[0], preferred_element_type=F32),
                          jnp.dot(p[half:], v_aug[1], preferred_element_type=F32)], axis=0)
    acc_ref[...] = alpha * acc_ref[...] + pv
    m_ref[...] = m_new


def _flash_result(acc_h):
    return acc_h / pltpu.roll(acc_h, HEAD_DIM, axis=1)


def _rank_select(score, cand, b_idx, n_blocks, k):
    rank = jnp.zeros(score.shape, F32)
    for bp in range(n_blocks):
        col = score[..., bp:bp + 1]
        tie = jnp.where(b_idx > bp, 1.0, 0.0)
        rank = rank + jnp.where(col > score, 1.0, jnp.where(col == score, tie, 0.0))
    return jnp.where(cand, jnp.where(rank < float(k), 1.0, 0.0), 0.0)


def _heads_to_lanes(read_rows, n_groups, r):
    hpg = N_HEADS // n_groups
    outs = []
    for hh in range(N_HEADS):
        g = hh // hpg
        outs.append(read_rows(hh)[:, g * HEAD_DIM:(g + 1) * HEAD_DIM])
    return jnp.concatenate(outs, axis=1)


def _nsa_prompt_kernel(q_ref, kcs_ref, vcs_ref, ks_ref, vs_ref, kw_ref, vw_ref, gl_ref, bg_ref,
                       tab_ref, bc_ref, ex_ref, o_ref,
                       m_ref, acc_ref, oc_ref, os_ref, sel_ref, *, nb):
    j = pl.program_id(1)
    g_n = NSA_KV_HEADS
    hpg = N_HEADS // g_n
    rows = N_HEADS * Q_BLOCK
    qbd = _build_qbd(q_ref[...].astype(F32), g_n)

    s_c = _dot_nt(qbd, kcs_ref[...].astype(BF16)) + bc_ref[0]
    qi = lax.broadcasted_iota(jnp.int32, (Q_BLOCK, nb), 0)
    bi = lax.broadcasted_iota(jnp.int32, (Q_BLOCK, nb), 1)
    t = j * Q_BLOCK + qi
    valid_c = (t - (bi + 1) * NSA_CMP_BLOCK + 1) >= 0
    e_c = jnp.exp(s_c - jnp.max(s_c, axis=1, keepdims=True))
    p_c = e_c / jnp.sum(e_c, axis=1, keepdims=True)
    p_c4 = jnp.where(valid_c[None, None], p_c.reshape(g_n, hpg, Q_BLOCK, nb), 0.0)
    oc_ref[...] = jnp.dot(p_c4.reshape(rows, nb).astype(BF16), vcs_ref[...].astype(BF16),
                          preferred_element_type=F32)

    imp = jnp.sum(p_c4, axis=1)
    cur = t // NSA_CMP_BLOCK
    forced = (bi == cur) | (bi == cur - 1) | (bi == 0)
    cand = bi <= cur
    for g in range(g_n):
        score = jnp.where(cand, jnp.where(forced, NSA_FORCED_SCORE, imp[g]), -1.0)
        sel_ref[g] = _rank_select(score, cand, bi, nb, min(NSA_TOPK, nb)).astype(BF16)

    def reset():
        m_ref[...] = jnp.full(m_ref.shape, NEG_INF, F32)
        acc_ref[...] = jnp.zeros(acc_ref.shape, F32)

    def masked(s, e_t):
        w = e_t.shape[1]
        msk = jnp.stack([jnp.dot(sel_ref[g], e_t, preferred_element_type=F32) for g in range(g_n)])
        madd = (1.0 - msk) * NEG_INF
        return (s.reshape(g_n, hpg, Q_BLOCK, w) + madd[:, None]).reshape(rows, w)

    reset()

    def sel_far(c, carry):
        off = pl.multiple_of(c * 2 * Q_BLOCK, 2 * Q_BLOCK)
        e_t = jnp.concatenate([ex_ref[2 * c], ex_ref[2 * c + 1]], axis=1)
        s = masked(_dot_nt(qbd, ks_ref[pl.ds(off, 2 * Q_BLOCK), :]), e_t)
        _flash_update(s, vs_ref[pl.ds(off, 2 * Q_BLOCK), :], m_ref, acc_ref)
        return carry

    n_pair = jnp.maximum(j - 1, 0) // 2
    lax.fori_loop(0, n_pair, sel_far, 0)

    def sel_near(kt, carry):
        off = pl.multiple_of(kt * Q_BLOCK, Q_BLOCK)
        s = _dot_nt(qbd, ks_ref[pl.ds(off, Q_BLOCK), :]) + tab_ref[jnp.minimum(j - kt, 2)]
        _flash_update(masked(s, ex_ref[kt]), vs_ref[pl.ds(off, Q_BLOCK), :], m_ref, acc_ref)
        return carry

    lax.fori_loop(2 * n_pair, j + 1, sel_near, 0)
    os_ref[...] = acc_ref[...]

    reset()

    def win_tile(i, carry):
        off = pl.multiple_of((j - i) * Q_BLOCK, Q_BLOCK)
        s = _dot_nt(qbd, kw_ref[pl.ds(off, Q_BLOCK), :]) + tab_ref[i]
        _flash_update(s, vw_ref[pl.ds(off, Q_BLOCK), :], m_ref, acc_ref)
        return carry

    lax.fori_loop(0, jnp.minimum(j, NSA_WINDOW // Q_BLOCK) + 1, win_tile, 0)

    sig = jax.nn.sigmoid(gl_ref[...] + bg_ref[...])

    def head_out(hh):
        r0 = hh * Q_BLOCK
        return (sig[:, 3 * hh:3 * hh + 1] * oc_ref[r0:r0 + Q_BLOCK, :]
                + sig[:, 3 * hh + 1:3 * hh + 2] * _flash_result(os_ref[r0:r0 + Q_BLOCK, :])
                + sig[:, 3 * hh + 2:3 * hh + 3] * _flash_result(acc_ref[r0:r0 + Q_BLOCK, :]))

    o_ref[...] = _heads_to_lanes(head_out, g_n, Q_BLOCK).astype(o_ref.dtype)


def _nsa_prompt_attn(y, yb, kcs, vcs, b_gate_row, tabs, bc, ex, n_batch, seq):
    nq = seq // Q_BLOCK
    nb = seq // NSA_CMP_BLOCK
    rows = N_HEADS * Q_BLOCK
    c = NSA_COLS
    kv = lambda col: pl.BlockSpec((seq, LANES), lambda n, j: (n, col))
    const = lambda shape: pl.BlockSpec(shape, lambda n, j: (0,) * len(shape))
    stacked = pltpu.VMEM((rows, LANES), F32)
    return pl.pallas_call(
        functools.partial(_nsa_prompt_kernel, nb=nb),
        grid=(n_batch, nq),
        in_specs=[
            pl.BlockSpec((Q_BLOCK, D_MODEL), lambda n, j: (n * nq + j, 0)),
            pl.BlockSpec((nb, LANES), lambda n, j: (n, 0)),
            pl.BlockSpec((nb, LANES), lambda n, j: (n, 0)),
            kv(c["ks"]), kv(c["vs"]), kv(c["kw"]), kv(c["vw"]),
            pl.BlockSpec((Q_BLOCK, LANES), lambda n, j: (n * nq + j, c["gate"])),
            const((1, LANES)),
            const(tabs.shape),
            pl.BlockSpec((1, rows, nb), lambda n, j: (j, 0, 0)),
            const(ex.shape),
        ],
        out_specs=pl.BlockSpec((Q_BLOCK, D_MODEL), lambda n, j: (n * nq + j, 0)),
        out_shape=jax.ShapeDtypeStruct((n_batch * seq, D_MODEL), BF16),
        scratch_shapes=[stacked, stacked, stacked, stacked, pltpu.VMEM((NSA_KV_HEADS, Q_BLOCK, nb), BF16)],
        compiler_params=_cparams("parallel", "arbitrary"),
        name="nsa_prompt",
    )(yb, kcs, vcs, yb, yb, yb, yb, y, b_gate_row, tabs, bc, ex)


def _suffix_sums(u, tri):
    u_hi = u.astype(BF16)
    u_lo = (u - u_hi.astype(F32)).astype(BF16)
    return (jnp.dot(u_hi, tri, preferred_element_type=F32) + jnp.dot(u_lo, tri, preferred_element_type=F32))


def _stick_break_tile(z, v_t, tri, c_prev, valid):
    sp = jnp.log1p(jnp.exp(-jnp.abs(z)))
    ls = jnp.minimum(z, 0.0) - sp
    u = -jnp.maximum(z, 0.0) - sp
    if valid is not None:
        u = jnp.where(valid, u, 0.0)
    a = jnp.exp(ls + _suffix_sums(u, tri) + c_prev)
    if valid is not None:
        a = jnp.where(valid, a, 0.0)
    return (jnp.dot(a.astype(BF16), v_t, preferred_element_type=F32),
            c_prev + jnp.sum(u, axis=1, keepdims=True))


def _sb_prompt_kernel(q_ref, k_ref, v_ref, tri_ref, o_ref, acc_ref, c_ref):
    j = pl.program_id(1)
    g_n = SB_KV_HEADS
    rows = N_HEADS * Q_BLOCK
    qbd = _build_qbd(q_ref[...].astype(F32), g_n)
    acc_ref[...] = jnp.zeros(acc_ref.shape, F32)
    c_ref[...] = jnp.zeros(c_ref.shape, F32)

    def tile(off, valid):
        z = _dot_nt(qbd, k_ref[pl.ds(off, Q_BLOCK), :])
        av, c_new = _stick_break_tile(z, v_ref[pl.ds(off, Q_BLOCK), :], tri_ref[...], c_ref[...], valid)
        acc_ref[...] += av
        c_ref[...] = c_new

    qi = lax.broadcasted_iota(jnp.int32, (rows, Q_BLOCK), 0) % Q_BLOCK
    ki = lax.broadcasted_iota(jnp.int32, (rows, Q_BLOCK), 1)
    tile(pl.multiple_of(j * Q_BLOCK, Q_BLOCK), ki < qi)

    def body(i, carry):
        tile(pl.multiple_of((j - i) * Q_BLOCK, Q_BLOCK), None)
        return carry

    lax.fori_loop(1, j + 1, body, 0)
    o_ref[...] = _heads_to_lanes(lambda hh: acc_ref[hh * Q_BLOCK:(hh + 1) * Q_BLOCK, :], g_n,
                                 Q_BLOCK).astype(o_ref.dtype)


def _tri_strict_lower():
    qi = jnp.arange(Q_BLOCK)[:, None]
    ki = jnp.arange(Q_BLOCK)[None, :]
    return (qi > ki).astype(BF16)


def _sb_prompt_attn(yb, n_batch, seq):
    nq = seq // Q_BLOCK
    kvw = SB_KV_HEADS * HEAD_DIM
    rows = N_HEADS * Q_BLOCK
    return pl.pallas_call(
        _sb_prompt_kernel,
        grid=(n_batch, nq),
        in_specs=[
            pl.BlockSpec((Q_BLOCK, D_MODEL), lambda n, j: (n * nq + j, 0)),
            pl.BlockSpec((seq, kvw), lambda n, j: (n, D_MODEL // kvw)),
            pl.BlockSpec((seq, kvw), lambda n, j: (n, D_MODEL // kvw + 1)),
            pl.BlockSpec((Q_BLOCK, Q_BLOCK), lambda n, j: (0, 0)),
        ],
        out_specs=pl.BlockSpec((Q_BLOCK, D_MODEL), lambda n, j: (n * nq + j, 0)),
        out_shape=jax.ShapeDtypeStruct((n_batch * seq, D_MODEL), BF16),
        scratch_shapes=[pltpu.VMEM((rows, kvw), F32), pltpu.VMEM((rows, Q_BLOCK), F32)],
        compiler_params=_cparams("parallel", "arbitrary"),
        name="sb_prompt",
    )(yb, yb, yb, _tri_strict_lower())


def _swa_prompt_kernel(q_ref, k_ref, v_ref, tab_ref, sink_ref, o_ref, m_ref, acc_ref):
    j = pl.program_id(1)
    g_n = SWA_KV_HEADS
    rows = N_HEADS * Q_BLOCK
    qbd = _build_qbd(q_ref[...].astype(F32), g_n)
    m_ref[...] = sink_ref[...]
    lane = lax.broadcasted_iota(jnp.int32, (rows, LANES), 1)
    row = lax.broadcasted_iota(jnp.int32, (rows, LANES), 0)
    own = (lane >= HEAD_DIM) == (row >= rows // g_n)
    acc_ref[...] = jnp.where(own, 0.0, 1.0)

    def body(i, carry):
        off = pl.multiple_of((j - i) * Q_BLOCK, Q_BLOCK)
        s = _dot_nt(qbd, k_ref[pl.ds(off, Q_BLOCK), :]) + tab_ref[i]
        _flash_update(s, v_ref[pl.ds(off, Q_BLOCK), :], m_ref, acc_ref)
        return carry

    lax.fori_loop(0, jnp.minimum(j, SWA_WINDOW // Q_BLOCK) + 1, body, 0)
    o_ref[...] = _heads_to_lanes(lambda hh: _flash_result(acc_ref[hh * Q_BLOCK:(hh + 1) * Q_BLOCK, :]), g_n,
                                 Q_BLOCK).astype(o_ref.dtype)


def _swa_prompt_attn(yb, tabs, sink_rows, n_batch, seq):
    nq = seq // Q_BLOCK
    rows = N_HEADS * Q_BLOCK
    c = SWA_COLS
    return pl.pallas_call(
        _swa_prompt_kernel,
        grid=(n_batch, nq),
        in_specs=[
            pl.BlockSpec((Q_BLOCK, D_MODEL), lambda n, j: (n * nq + j, 0)),
            pl.BlockSpec((seq, LANES), lambda n, j: (n, c["k"])),
            pl.BlockSpec((seq, LANES), lambda n, j: (n, c["v"])),
            pl.BlockSpec(tabs.shape, lambda n, j: (0, 0, 0)),
            pl.BlockSpec((rows, LANES), lambda n, j: (0, 0)),
        ],
        out_specs=pl.BlockSpec((Q_BLOCK, D_MODEL), lambda n, j: (n * nq + j, 0)),
        out_shape=jax.ShapeDtypeStruct((n_batch * seq, D_MODEL), BF16),
        scratch_shapes=[pltpu.VMEM((rows, LANES), F32), pltpu.VMEM((rows, LANES), F32)],
        compiler_params=_cparams("parallel", "arbitrary"),
        name="swa_prompt",
    )(yb, yb, yb, tabs, sink_rows)


def _pages_t(pool):
    l, n_pool, page, g, hd = pool.shape
    return jnp.transpose(pool, (0, 1, 3, 4, 2)).reshape(l * n_pool, g * hd, page)


def _buf_t(buf):
    n, w, g, hd = buf.shape
    return jnp.transpose(buf, (0, 2, 3, 1)).reshape(n, g * hd, w)


def _pad_rows(x, n):
    return jnp.concatenate([x, jnp.zeros((n - x.shape[0], x.shape[1]), x.dtype)], axis=0)


def _pool_compress_kernel(x_ref, w_ref, o_ref, *, n_pages, g_n):
    kvw = g_n * HEAD_DIM
    accs = [jnp.zeros((n_pages, LANES), F32) for _ in range(g_n)]
    for r in range(kvw):
        x_r = x_ref[pl.ds(r, n_pages, stride=kvw), :].astype(BF16)
        accs[r // HEAD_DIM] = accs[r // HEAD_DIM] + jnp.dot(x_r, w_ref[r], preferred_element_type=F32)
    o_ref[...] = jnp.concatenate(accs, axis=1)


def _pool_compress(pool_t2d, w_rows, slot, n_pool, g_n):
    kvw = g_n * HEAD_DIM
    n_pages = next(c for c in (POOL_PAGES_PER_STEP, 64, 32, 16, 8) if n_pool % c == 0)
    steps = n_pool // n_pages
    return pl.pallas_call(
        functools.partial(_pool_compress_kernel, n_pages=n_pages, g_n=g_n),
        grid=(steps,),
        in_specs=[
            pl.BlockSpec((n_pages * kvw, PAGE_SIZE), lambda i: (i + slot * steps, 0)),
            pl.BlockSpec(w_rows.shape, lambda i: (0, 0, 0)),
        ],
        out_specs=pl.BlockSpec((n_pages, g_n * LANES), lambda i: (i, 0)),
        out_shape=jax.ShapeDtypeStruct((n_pool, g_n * LANES), F32),
        compiler_params=_cparams("parallel"),
        name="pool_compress",
    )(pool_t2d, w_rows)


def _nsa_sample_kernel(pt_ref, q_ref, ksn_ref, vsn_ref, kwn_ref, vwn_ref, gl_ref, bg_ref,
                       kcs_ref, vcs_ref, bcs_ref, tlast_ref, tfar_ref, tnew_ref, wtb_ref, ex_ref,
                       kwb_ref, vwb_ref, *rest, ppc, nchunk, nb, t_new):
    k_pages = rest[:ppc]
    v_pages = rest[ppc:2 * ppc]
    o_ref = rest[2 * ppc]
    m_ref, l_ref, acc_ref, oc_ref, ow_ref, sel_ref, qbd_ref = rest[2 * ppc + 1:]
    c = pl.program_id(1)
    g_n = NSA_KV_HEADS
    hpg = N_HEADS // g_n
    rows = N_HEADS * t_new
    nbp = kcs_ref.shape[0]

    @pl.when(c == 0)
    def _():
        qbd = _build_qbd(q_ref[...], g_n)
        qbd_ref[...] = qbd
        s_c = _dot_nt(qbd, kcs_ref[...].astype(BF16)) + bcs_ref[...]
        bi = lax.broadcasted_iota(jnp.int32, (rows, nbp), 1)
        e_c = jnp.exp(s_c - jnp.max(s_c, axis=1, keepdims=True))
        p_c = jnp.where(bi < nb - 1, e_c / jnp.sum(e_c, axis=1, keepdims=True), 0.0)
        oc_ref[...] = jnp.dot(p_c.astype(BF16), vcs_ref[...].astype(BF16), preferred_element_type=F32)
        imp = jnp.sum(p_c.reshape(g_n, hpg, t_new, nbp), axis=1)
        b3 = lax.broadcasted_iota(jnp.int32, (g_n, t_new, nbp), 2)
        forced = (b3 == nb - 1) | (b3 == nb - 2) | (b3 == 0)
        cand = b3 <= nb - 1
        score = jnp.where(cand, jnp.where(forced, NSA_FORCED_SCORE, imp), -1.0)
        sel = _rank_select(score, cand, b3, nb, min(NSA_TOPK, nb))
        sel_ref[...] = jnp.broadcast_to(sel[:, None], (g_n, hpg, t_new, nbp)).reshape(rows, nbp).astype(BF16)

        s_b = jnp.dot(qbd, kwb_ref[...].astype(BF16), preferred_element_type=F32) + wtb_ref[...]
        s_n = _dot_nt(qbd, _pad_rows(kwn_ref[...], LANES).astype(BF16)) + tnew_ref[...]
        m_w = jnp.maximum(jnp.max(s_b, axis=1, keepdims=True), jnp.max(s_n, axis=1, keepdims=True))
        p_b = jnp.exp(s_b - m_w)
        p_n = jnp.exp(s_n - m_w)
        l_w = jnp.sum(p_b, axis=1, keepdims=True) + jnp.sum(p_n, axis=1, keepdims=True)
        o_w = (_dot_nt(p_b.astype(BF16), vwb_ref[...].astype(BF16))
               + jnp.dot(p_n.astype(BF16), _pad_rows(vwn_ref[...], LANES).astype(BF16),
                         preferred_element_type=F32))
        ow_ref[...] = o_w / l_w

        s0 = _dot_nt(qbd, _pad_rows(ksn_ref[...], LANES).astype(BF16)) + tnew_ref[...]
        m0 = jnp.max(s0, axis=1, keepdims=True)
        p0 = jnp.exp(s0 - m0)
        m_ref[...] = jnp.broadcast_to(m0, m_ref.shape)
        l_ref[...] = jnp.broadcast_to(jnp.sum(p0, axis=1, keepdims=True), l_ref.shape)
        acc_ref[...] = jnp.dot(p0.astype(BF16), _pad_rows(vsn_ref[...], LANES).astype(BF16),
                               preferred_element_type=F32)

    qbd = qbd_ref[...]
    msk = jnp.dot(sel_ref[...], ex_ref[...], preferred_element_type=F32)
    parts = []
    for r in range(ppc):
        s_r = jnp.dot(qbd, k_pages[r][...].astype(BF16), preferred_element_type=F32)
        if r == ppc - 1:
            s_r = s_r + jnp.where(c == nchunk - 1, tlast_ref[...], tfar_ref[...])
        else:
            s_r = s_r + tfar_ref[...]
        parts.append(s_r)
    s = jnp.concatenate(parts, axis=1) + (1.0 - msk) * NEG_INF
    m_prev = m_ref[...]
    m_new = jnp.maximum(m_prev, jnp.max(s, axis=1, keepdims=True))
    alpha = jnp.exp(m_prev - m_new)
    p = jnp.exp(s - m_new[:, 0:1])
    pv = _dot_nt(p[:, 0:PAGE_SIZE].astype(BF16), v_pages[0][...].astype(BF16))
    for r in range(1, ppc):
        pv = pv + _dot_nt(p[:, r * PAGE_SIZE:(r + 1) * PAGE_SIZE].astype(BF16), v_pages[r][...].astype(BF16))
    l_ref[...] = alpha * l_ref[...] + jnp.sum(p, axis=1, keepdims=True)
    acc_ref[...] = alpha * acc_ref[...] + pv
    m_ref[...] = m_new

    @pl.when(c == nchunk - 1)
    def _():
        sig = jax.nn.sigmoid(gl_ref[...] + bg_ref[...])

        def head_out(hh):
            r0 = hh * t_new
            o_s = acc_ref[r0:r0 + t_new, :] / l_ref[r0:r0 + t_new, :]
            return (sig[:, 3 * hh:3 * hh + 1] * oc_ref[r0:r0 + t_new, :]
                    + sig[:, 3 * hh + 1:3 * hh + 2] * o_s
                    + sig[:, 3 * hh + 2:3 * hh + 3] * ow_ref[r0:r0 + t_new, :])

        o_ref[...] = _heads_to_lanes(head_out, g_n, t_new).astype(o_ref.dtype)


def _nsa_sample_tables(rel_bias, past, t_new, wb, nb, nbp):
    i = jnp.arange(t_new)[:, None]
    ki = jnp.arange(LANES)[None, :]
    d_new = i - ki
    tnew = _bias_rows(rel_bias, d_new, (d_new >= 0) & (ki < t_new), t_new)
    d_last = PAGE_SIZE + i - ki
    tlast = _bias_rows(rel_bias, d_last, d_last >= 0, t_new)
    d_far = jnp.full((t_new, LANES), 2 * PAGE_SIZE)
    tfar = _bias_rows(rel_bias, d_far, d_far >= 0, t_new)
    r = jnp.arange(wb)[None, :]
    d_w = wb + i - r
    wtb = _bias_rows(rel_bias, d_w, (d_w <= NSA_WINDOW) & (past - wb + r >= 0), t_new)
    b = jnp.arange(nbp)[None, :]
    d_c = past + i - ((b + 1) * NSA_CMP_BLOCK - 1)
    bcs = _bias_rows(rel_bias, d_c, (d_c >= 0) & (b < nb), t_new)
    return tnew, tlast, tfar, wtb, bcs


def _nsa_sample_attn(y, kcs, vcs, b_gate_row, tables, ex, kwb_t, vwb_t, pool_k_t, pool_v_t, pt_flat,
                     n_seq, t_new, n_pages, nb):
    ppc = PAGES_PER_STEP
    nchunk = n_pages // ppc
    rows = N_HEADS * t_new
    nbp = kcs.shape[1]
    wb = kwb_t.shape[2]
    tnew, tlast, tfar, wtb, bcs = tables
    c = NSA_COLS
    new = lambda col: pl.BlockSpec((t_new, LANES), lambda n, ch, pt: (n, col))
    const = lambda shape: pl.BlockSpec(shape, lambda n, ch, pt: (0,) * len(shape))

    def page_spec(r):
        return pl.BlockSpec((None, LANES, PAGE_SIZE),
                            lambda n, ch, pt: (pt[n * n_pages + ch * ppc + r], 0, 0))

    in_specs = [
        pl.BlockSpec((t_new, D_MODEL), lambda n, ch, pt: (n, 0)),
        new(c["ks"]), new(c["vs"]), new(c["kw"]), new(c["vw"]), new(c["gate"]),
        const((1, LANES)),
        pl.BlockSpec((None, nbp, LANES), lambda n, ch, pt: (n, 0, 0)),
        pl.BlockSpec((None, nbp, LANES), lambda n, ch, pt: (n, 0, 0)),
        const((rows, nbp)), const((rows, LANES)), const((rows, LANES)), const((rows, LANES)),
        const((rows, wb)),
        pl.BlockSpec((nbp, ppc * PAGE_SIZE), lambda n, ch, pt: (0, ch)),
        pl.BlockSpec((None, LANES, wb), lambda n, ch, pt: (n, 0, 0)),
        pl.BlockSpec((None, LANES, wb), lambda n, ch, pt: (n, 0, 0)),
    ] + [page_spec(r) for r in range(ppc)] + [page_spec(r) for r in range(ppc)]
    grid_spec = pltpu.PrefetchScalarGridSpec(
        num_scalar_prefetch=1,
        grid=(n_seq, nchunk),
        in_specs=in_specs,
        out_specs=pl.BlockSpec((t_new, D_MODEL), lambda n, ch, pt: (n, 0)),
        scratch_shapes=[pltpu.VMEM((rows, LANES), F32)] * 5 + [pltpu.VMEM((rows, nbp), BF16),
                                                               pltpu.VMEM((rows, LANES), BF16)],
    )
    return pl.pallas_call(
        functools.partial(_nsa_sample_kernel, ppc=ppc, nchunk=nchunk, nb=nb, t_new=t_new),
        grid_spec=grid_spec,
        out_shape=jax.ShapeDtypeStruct((n_seq * t_new, D_MODEL), BF16),
        compiler_params=_cparams("parallel", "arbitrary"),
        name="nsa_sample",
    )(pt_flat, y, y, y, y, y, y, b_gate_row, kcs, vcs, bcs, tlast, tfar, tnew, wtb, ex, kwb_t, vwb_t,
      *([pool_k_t] * ppc), *([pool_v_t] * ppc))


def _sb_sample_kernel(pt_ref, q_ref, kn_ref, vn_ref, tri_ref, *rest, ppc, t_new):
    k_pages = rest[:ppc]
    v_pages = rest[ppc:2 * ppc]
    o_ref = rest[2 * ppc]
    qbd_ref, acc_ref, c_ref = rest[2 * ppc + 1:]
    c = pl.program_id(1)
    g_n = SB_KV_HEADS
    rows = N_HEADS * t_new

    @pl.when(c == 0)
    def _():
        qbd0 = _build_qbd(q_ref[...], g_n)
        qbd_ref[...] = qbd0
        ki = lax.broadcasted_iota(jnp.int32, (rows, PAGE_SIZE), 1)
        qi = lax.broadcasted_iota(jnp.int32, (rows, PAGE_SIZE), 0) % t_new
        z = _dot_nt(qbd0, _pad_rows(kn_ref[...], PAGE_SIZE).astype(BF16))
        av, c_new = _stick_break_tile(z, _pad_rows(vn_ref[...], PAGE_SIZE).astype(BF16), tri_ref[...],
                                      jnp.zeros((rows, PAGE_SIZE), F32), ki < qi)
        acc_ref[...] = av
        c_ref[...] = c_new

    qbd = qbd_ref[...]
    tri = tri_ref[...]
    c_run = c_ref[...]
    acc = acc_ref[...]
    for r in reversed(range(ppc)):
        z = jnp.dot(qbd, k_pages[r][...].astype(BF16), preferred_element_type=F32)
        sp = jnp.log1p(jnp.exp(-jnp.abs(z)))
        ls = jnp.minimum(z, 0.0) - sp
        u = -jnp.maximum(z, 0.0) - sp
        a = jnp.exp(ls + _suffix_sums(u, tri) + c_run)
        acc = acc + _dot_nt(a.astype(BF16), v_pages[r][...].astype(BF16))
        c_run = c_run + jnp.sum(u, axis=1, keepdims=True)
    acc_ref[...] = acc
    c_ref[...] = c_run

    @pl.when(c == pl.num_programs(1) - 1)
    def _():
        o_ref[...] = _heads_to_lanes(lambda hh: acc_ref[hh * t_new:(hh + 1) * t_new, :], g_n,
                                     t_new).astype(o_ref.dtype)


def _sb_sample_attn(y, pool_k_t, pool_v_t, pt_flat, n_seq, t_new, n_pages):
    ppc = PAGES_PER_STEP
    nchunk = n_pages // ppc
    rows = N_HEADS * t_new
    kvw = SB_KV_HEADS * HEAD_DIM

    def page_spec(r):
        return pl.BlockSpec((None, kvw, PAGE_SIZE),
                            lambda n, ch, pt: (pt[n * n_pages + (nchunk - 1 - ch) * ppc + r], 0, 0))

    in_specs = [
        pl.BlockSpec((t_new, D_MODEL), lambda n, ch, pt: (n, 0)),
        pl.BlockSpec((t_new, kvw), lambda n, ch, pt: (n, D_MODEL // kvw)),
        pl.BlockSpec((t_new, kvw), lambda n, ch, pt: (n, D_MODEL // kvw + 1)),
        pl.BlockSpec((PAGE_SIZE, PAGE_SIZE), lambda n, ch, pt: (0, 0)),
    ] + [page_spec(r) for r in range(ppc)] + [page_spec(r) for r in range(ppc)]
    grid_spec = pltpu.PrefetchScalarGridSpec(
        num_scalar_prefetch=1,
        grid=(n_seq, nchunk),
        in_specs=in_specs,
        out_specs=pl.BlockSpec((t_new, D_MODEL), lambda n, ch, pt: (n, 0)),
        scratch_shapes=[pltpu.VMEM((rows, kvw), BF16), pltpu.VMEM((rows, kvw), F32),
                        pltpu.VMEM((rows, PAGE_SIZE), F32)],
    )
    return pl.pallas_call(
        functools.partial(_sb_sample_kernel, ppc=ppc, t_new=t_new),
        grid_spec=grid_spec,
        out_shape=jax.ShapeDtypeStruct((n_seq * t_new, D_MODEL), BF16),
        compiler_params=_cparams("parallel", "arbitrary"),
        name="sb_sample",
    )(pt_flat, y, y, y, _tri_strict_lower(), *([pool_k_t] * ppc), *([pool_v_t] * ppc))


def _swa_sample_kernel(q_ref, kn_ref, vn_ref, kb_ref, vb_ref, tb_ref, tn_ref, sink_ref, o_ref, *, t_new):
    qbd = _build_qbd(q_ref[...], SWA_KV_HEADS)
    s_b = jnp.dot(qbd, kb_ref[...].astype(BF16), preferred_element_type=F32) + tb_ref[...]
    s_n = _dot_nt(qbd, _pad_rows(kn_ref[...], LANES).astype(BF16)) + tn_ref[...]
    sink = sink_ref[...][:, 0:1]
    m = jnp.maximum(jnp.maximum(jnp.max(s_b, axis=1, keepdims=True), jnp.max(s_n, axis=1, keepdims=True)), sink)
    p_b = jnp.exp(s_b - m)
    p_n = jnp.exp(s_n - m)
    l = jnp.sum(p_b, axis=1, keepdims=True) + jnp.sum(p_n, axis=1, keepdims=True) + jnp.exp(sink - m)
    o = (_dot_nt(p_b.astype(BF16), vb_ref[...].astype(BF16))
         + jnp.dot(p_n.astype(BF16), _pad_rows(vn_ref[...], LANES).astype(BF16),
                   preferred_element_type=F32)) / l
    o_ref[...] = _heads_to_lanes(lambda hh: o[hh * t_new:(hh + 1) * t_new, :], SWA_KV_HEADS,
                                 t_new).astype(o_ref.dtype)


def _swa_sample_attn(y, kb_t, vb_t, rel_bias, sinks, past, n_seq, t_new):
    rows = N_HEADS * t_new
    wb = kb_t.shape[2]
    i = jnp.arange(t_new)[:, None]
    r = jnp.arange(wb)[None, :]
    d_b = wb + i - r
    tb = _bias_rows(rel_bias, d_b, (d_b <= SWA_WINDOW) & (past - wb + r >= 0), t_new)
    ki = jnp.arange(LANES)[None, :]
    d_n = i - ki
    tn = _bias_rows(rel_bias, d_n, (d_n >= 0) & (ki < t_new), t_new)
    sink_rows = jnp.broadcast_to(jnp.repeat(sinks.astype(F32), t_new)[:, None], (rows, LANES))
    c = SWA_COLS
    return pl.pallas_call(
        functools.partial(_swa_sample_kernel, t_new=t_new),
        grid=(n_seq,),
        in_specs=[
            pl.BlockSpec((t_new, D_MODEL), lambda n: (n, 0)),
            pl.BlockSpec((t_new, LANES), lambda n: (n, c["k"])),
            pl.BlockSpec((t_new, LANES), lambda n: (n, c["v"])),
            pl.BlockSpec((None, LANES, wb), lambda n: (n, 0, 0)),
            pl.BlockSpec((None, LANES, wb), lambda n: (n, 0, 0)),
            pl.BlockSpec((rows, wb), lambda n: (0, 0)),
            pl.BlockSpec((rows, LANES), lambda n: (0, 0)),
            pl.BlockSpec((rows, LANES), lambda n: (0, 0)),
        ],
        out_specs=pl.BlockSpec((t_new, D_MODEL), lambda n: (n, 0)),
        out_shape=jax.ShapeDtypeStruct((n_seq * t_new, D_MODEL), BF16),
        compiler_params=_cparams("parallel"),
        name="swa_sample",
    )(y, y, y, kb_t, vb_t, tb, tn, sink_rows)


def _pad_cols(w, n):
    return jnp.pad(w, ((0, 0), (0, n - w.shape[1])))


def _compress_weight(w_phi):
    g_n = w_phi.shape[0]
    wp = jnp.transpose(w_phi, (1, 0, 2, 3))
    return jnp.einsum("jgde,gh->jgdhe", wp, jnp.eye(g_n, dtype=w_phi.dtype)).reshape(
        NSA_CMP_BLOCK * g_n * HEAD_DIM, g_n * HEAD_DIM)


def _compress_weight_rows(w_phi):
    bpp = PAGE_SIZE // NSA_CMP_BLOCK
    g_n = w_phi.shape[0]
    w = jnp.einsum("gjde,bc->gdbjce", w_phi, jnp.eye(bpp, dtype=w_phi.dtype))
    return w.reshape(g_n * HEAD_DIM, PAGE_SIZE, bpp * HEAD_DIM).astype(BF16)


def _expand_blocks(n_blocks_padded, n_keys):
    b = jnp.arange(n_blocks_padded)[:, None]
    k = jnp.arange(n_keys)[None, :]
    return (b == k // NSA_CMP_BLOCK).astype(BF16)


def _nsa_layer(xp, xs, mod_p, mod_s, slot, n_batch, seq, n_seq, t_new, pools, bufs, page_table, params,
               rel_bias, band_tabs):
    w_in, b_gate, w_phi_k, w_phi_v = params
    g_n = NSA_KV_HEADS
    kvw = g_n * HEAD_DIM
    w = _pad_cols(w_in, NSA_NTOT).astype(BF16)
    bias = jnp.zeros((NSA_NTOT,), F32)
    yp, ypb = _modmm(xp, mod_p[0], mod_p[1], w, bias, seq, second="b")
    ys, _ = _modmm(xs, mod_s[0], mod_s[1], w, bias, t_new, second="b")
    b_gate_row = jnp.pad(b_gate.astype(F32), (0, LANES - b_gate.shape[0])).reshape(1, LANES)
    wk = _compress_weight(w_phi_k)
    wv = _compress_weight(w_phi_v)
    col = lambda y, name: y[:, NSA_COLS[name] * LANES:(NSA_COLS[name] + 1) * LANES]

    nq = seq // Q_BLOCK
    nb = seq // NSA_CMP_BLOCK
    kc_p, vc_p = col(yp, "kc"), col(yp, "vc")
    flat_k = NSA_CMP_BLOCK * kvw
    kcs_p = _mm(kc_p.reshape(n_batch * nb, flat_k), wk, tm=min(256, n_batch * nb), tn=kvw, tk=2048)
    vcs_p = _mm(vc_p.reshape(n_batch * nb, flat_k), wv, tm=min(256, n_batch * nb), tn=kvw, tk=2048)
    t = jnp.arange(seq)[:, None]
    c_end = (jnp.arange(nb)[None, :] + 1) * NSA_CMP_BLOCK - 1
    d_c = t - c_end
    bc = _bias_heads(rel_bias, d_c, d_c >= 0).reshape(N_HEADS, nq, Q_BLOCK, nb).transpose(1, 0, 2, 3)
    bc = bc.reshape(nq, N_HEADS * Q_BLOCK, nb)
    ex = _expand_blocks(nb, seq).reshape(nb, nq, Q_BLOCK).transpose(1, 0, 2)
    op = _nsa_prompt_attn(yp, ypb, kcs_p, vcs_p, b_gate_row, band_tabs, bc, ex, n_batch, seq)

    n_pages = page_table.shape[1]
    past = n_pages * PAGE_SIZE
    pool_kc, pool_vc, pool_ks, pool_vs = pools
    n_pool = pool_kc.shape[1]
    nb_s = past // NSA_CMP_BLOCK + 1
    nbp = (nb_s + 15) // 16 * 16
    bpp = PAGE_SIZE // NSA_CMP_BLOCK

    def summaries(pool, new_rows, w_phi, wmat):
        s_all = _pool_compress(_pages_t(pool).reshape(-1, PAGE_SIZE), _compress_weight_rows(w_phi), slot,
                               n_pool, g_n)
        past_sum = s_all[page_table.reshape(-1)].reshape(n_seq, n_pages, g_n, bpp, HEAD_DIM)
        past_sum = jnp.transpose(past_sum, (0, 1, 3, 2, 4)).reshape(n_seq, n_pages * bpp, kvw)
        new_flat = new_rows.reshape(n_seq, t_new * kvw)
        last = _mm(new_flat, wmat[:t_new * kvw], tm=n_seq, tn=kvw, tk=t_new * kvw)
        pad = jnp.zeros((n_seq, nbp - nb_s, kvw), F32)
        return jnp.concatenate([past_sum, last[:, None, :], pad], axis=1)

    kcs_s = summaries(pool_kc, col(ys, "kc"), w_phi_k, wk)
    vcs_s = summaries(pool_vc, col(ys, "vc"), w_phi_v, wv)
    buf_kw, buf_vw = bufs
    wb = buf_kw.shape[1]
    tables = _nsa_sample_tables(rel_bias, past, t_new, wb, nb_s, nbp)
    ex_s = _expand_blocks(nbp, past)
    pt_flat = (page_table + slot * n_pool).reshape(-1).astype(jnp.int32)
    os_ = _nsa_sample_attn(ys, kcs_s, vcs_s, b_gate_row, tables, ex_s, _buf_t(buf_kw), _buf_t(buf_vw),
                           _pages_t(pool_ks), _pages_t(pool_vs), pt_flat, n_seq, t_new, n_pages, nb_s)

    shp_p = (n_batch, seq, NSA_KV_HEADS, HEAD_DIM)
    shp_s = (n_seq, t_new, NSA_KV_HEADS, HEAD_DIM)
    wbp = min(NSA_WINDOW, seq)
    st_p, st_s = [], []
    for name in ("kc", "vc", "ks", "vs"):
        st_p.append(col(yp, name).reshape(shp_p))
        st_s.append(col(ys, name).reshape(shp_s))
    for name, buf in (("kw", buf_kw), ("vw", buf_vw)):
        st_p.append(col(yp, name).reshape(shp_p)[:, seq - wbp:])
        st_s.append(jnp.concatenate([buf, col(ys, name).reshape(shp_s)], axis=1)[:, t_new:])
    return op, os_, st_p, st_s


def _sb_layer(xp, xs, mod_p, mod_s, slot, n_batch, seq, n_seq, t_new, pools, page_table, w_in):
    kvw = SB_KV_HEADS * HEAD_DIM
    w = w_in.astype(BF16)
    bias = jnp.zeros((SB_NTOT,), F32)
    yp, ypb = _modmm(xp, mod_p[0], mod_p[1], w, bias, seq, second="b")
    ys, _ = _modmm(xs, mod_s[0], mod_s[1], w, bias, t_new, second="b")
    op = _sb_prompt_attn(ypb, n_batch, seq)
    pool_k, pool_v = pools
    n_pool = pool_k.shape[1]
    n_pages = page_table.shape[1]
    pt_flat = (page_table + slot * n_pool).reshape(-1).astype(jnp.int32)
    os_ = _sb_sample_attn(ys, _pages_t(pool_k), _pages_t(pool_v), pt_flat, n_seq, t_new, n_pages)
    shp_p = (n_batch, seq, SB_KV_HEADS, HEAD_DIM)
    shp_s = (n_seq, t_new, SB_KV_HEADS, HEAD_DIM)
    st_p = [yp[:, D_MODEL:D_MODEL + kvw].reshape(shp_p), yp[:, D_MODEL + kvw:].reshape(shp_p)]
    st_s = [ys[:, D_MODEL:D_MODEL + kvw].reshape(shp_s), ys[:, D_MODEL + kvw:].reshape(shp_s)]
    return op, os_, st_p, st_s


def _swa_layer(xp, xs, mod_p, mod_s, n_batch, seq, n_seq, t_new, bufs, past, w_in, sinks, rel_bias):
    w = w_in.astype(BF16)
    bias = jnp.zeros((SWA_NTOT,), F32)
    yp, ypb = _modmm(xp, mod_p[0], mod_p[1], w, bias, seq, second="b")
    ys, _ = _modmm(xs, mod_s[0], mod_s[1], w, bias, t_new, second="b")
    rows = N_HEADS * Q_BLOCK
    tabs = _band_tables(rel_bias, SWA_WINDOW // Q_BLOCK + 1, SWA_WINDOW).reshape(-1, rows, Q_BLOCK)
    sink_rows = jnp.broadcast_to(jnp.repeat(sinks.astype(F32), Q_BLOCK)[:, None], (rows, LANES))
    op = _swa_prompt_attn(ypb, tabs, sink_rows, n_batch, seq)
    buf_k, buf_v = bufs
    os_ = _swa_sample_attn(ys, _buf_t(buf_k), _buf_t(buf_v), rel_bias, sinks, past, n_seq, t_new)
    shp_p = (n_batch, seq, SWA_KV_HEADS, HEAD_DIM)
    shp_s = (n_seq, t_new, SWA_KV_HEADS, HEAD_DIM)
    wbp = min(SWA_WINDOW, seq)
    col = lambda y, name: y[:, SWA_COLS[name] * LANES:(SWA_COLS[name] + 1) * LANES]
    st_p = [col(yp, n).reshape(shp_p)[:, seq - wbp:] for n in ("k", "v")]
    st_s = [jnp.concatenate([b, col(ys, n).reshape(shp_s)], axis=1)[:, t_new:]
            for n, b in (("k", buf_k), ("v", buf_v))]
    return op, os_, st_p, st_s


def _moe_layer(xp, xs, mod_p, mod_s, seq, t_new, layer, w_rg, b_rg, w_re, b_re, w_gate, w_up, w_down,
               ln_g, ln_b):
    tp = xp.shape[0]
    w_r = _pad_cols(jnp.concatenate([w_rg, w_re], axis=1), LANES).astype(BF16)
    b_r = jnp.pad(jnp.concatenate([b_rg, b_re]).astype(F32), (0, LANES - MOE_GROUPS - MOE_EXPERTS))
    lg_p, h_p = _modmm(xp, mod_p[0], mod_p[1], w_r, b_r, seq, second="h")
    lg_s, h_s = _modmm(xs, mod_s[0], mod_s[1], w_r, b_r, t_new, second="h")
    e_idx, e_w = _route(jnp.concatenate([lg_p, lg_s], axis=0))
    src, tile_e, tile_valid, dest = _moe_dispatch(e_idx, MOE_TILE)
    h_all = jnp.concatenate([h_p, h_s, jnp.zeros((8, D_MODEL), F32)], axis=0)
    out = _moe_ffn(h_all[src], tile_e + layer * MOE_EXPERTS, tile_valid, w_gate, w_up, w_down)
    f0, f1 = out[dest[:, 0]], out[dest[:, 1]]
    xp2 = _combine_res_ln(f0[:tp], f1[:tp], e_w[:tp], xp, mod_p[2], ln_g, ln_b, seq)
    xs2 = _combine_res_ln(f0[tp:], f1[tp:], e_w[tp:], xs, mod_s[2], ln_g, ln_b, t_new)
    return xp2, xs2


def kernel(x_prompt, x_sample, c_prompt, c_sample, cache_nsa_k_cmp, cache_nsa_v_cmp, cache_nsa_k_slc,
           cache_nsa_v_slc, cache_nsa_k_win, cache_nsa_v_win, cache_sb_k, cache_sb_v, cache_swa_k_win,
           cache_swa_v_win, page_table, rel_bias, w_ada, b_ada, ln1_g, ln1_b, ln2_g, ln2_b, w_in_nsa,
           b_gate_nsa, w_phi_k_nsa, w_phi_v_nsa, w_o_nsa, w_in_sb, w_o_sb, w_in_swa, sinks_swa, w_o_swa,
           w_route_group, b_route_group, w_route_expert, b_route_expert, w_exp_gate, w_exp_up, w_exp_down):
    n_batch, seq, _ = x_prompt.shape
    n_seq, t_new, _ = x_sample.shape
    past = page_table.shape[1] * PAGE_SIZE
    xp = x_prompt.reshape(n_batch * seq, D_MODEL)
    xs = x_sample.reshape(n_seq * t_new, D_MODEL)
    n_c = n_batch + n_seq
    n_c_pad = (n_c + 7) // 8 * 8
    c_all = jnp.concatenate([c_prompt, c_sample, jnp.zeros((n_c_pad - n_c, D_MODEL), F32)], axis=0)
    far_bias = rel_bias[REL_BUCKETS - 1].astype(F32)
    nsa_tabs = _band_tables(rel_bias, NSA_WINDOW // Q_BLOCK + 1, NSA_WINDOW, shift=far_bias).reshape(
        -1, N_HEADS * Q_BLOCK, Q_BLOCK)
    ts_tiles = (n_seq * t_new) // ROW_TILE
    w_gate_all = w_exp_gate.reshape(DEPTH * MOE_EXPERTS, D_MODEL, MOE_D_FF)
    w_up_all = w_exp_up.reshape(DEPTH * MOE_EXPERTS, D_MODEL, MOE_D_FF)
    w_down_all = w_exp_down.reshape(DEPTH * MOE_EXPERTS, MOE_D_FF, D_MODEL)

    st = {k: ([], []) for k in ("a", "b", "c")}
    for layer in range(DEPTH):
        kind, slot = layer % N_MIXERS, layer // N_MIXERS
        mods = _mm(c_all, w_ada[layer], b_ada[layer], act="silu", tm=n_c_pad, tn=1536, tk=D_MODEL)
        mp = [mods[:n_batch, k * D_MODEL:(k + 1) * D_MODEL].reshape(n_batch, 1, D_MODEL) for k in range(6)]
        ms = [jnp.repeat(mods[n_batch:n_c, k * D_MODEL:(k + 1) * D_MODEL], t_new, axis=0)
              .reshape(ts_tiles, ROW_TILE, D_MODEL) for k in range(6)]
        if kind == 0:
            op, os_, stp, sts = _nsa_layer(
                xp, xs, mp[0:2], ms[0:2], slot, n_batch, seq, n_seq, t_new,
                (cache_nsa_k_cmp, cache_nsa_v_cmp, cache_nsa_k_slc, cache_nsa_v_slc),
                (cache_nsa_k_win[slot], cache_nsa_v_win[slot]), page_table,
                (w_in_nsa[slot], b_gate_nsa[slot], w_phi_k_nsa[slot], w_phi_v_nsa[slot]), rel_bias, nsa_tabs)
            w_o, key = w_o_nsa[slot], "a"
        elif kind == 1:
            op, os_, stp, sts = _sb_layer(xp, xs, mp[0:2], ms[0:2], slot, n_batch, seq, n_seq, t_new,
                                          (cache_sb_k, cache_sb_v), page_table, w_in_sb[slot])
            w_o, key = w_o_sb[slot], "b"
        else:
            op, os_, stp, sts = _swa_layer(xp, xs, mp[0:2], ms[0:2], n_batch, seq, n_seq, t_new,
                                           (cache_swa_k_win[slot], cache_swa_v_win[slot]), past,
                                           w_in_swa[slot], sinks_swa[slot], rel_bias)
            w_o, key = w_o_swa[slot], "c"
        st[key][0].append(stp)
        st[key][1].append(sts)
        w_o = w_o.astype(BF16)
        xp = _proj_res_ln(op, xp, mp[2], w_o, ln1_g[layer], ln1_b[layer], seq)
        xs = _proj_res_ln(os_, xs, ms[2], w_o, ln1_g[layer], ln1_b[layer], t_new)
        xp, xs = _moe_layer(xp, xs, mp[3:6], ms[3:6], seq, t_new, layer, w_route_group[layer],
                            b_route_group[layer], w_route_expert[layer], b_route_expert[layer],
                            w_gate_all, w_up_all, w_down_all, ln2_g[layer], ln2_b[layer])

    outs = [xp.reshape(n_batch, seq, D_MODEL), xs.reshape(n_seq, t_new, D_MODEL)]
    for key, n_state in (("a", 6), ("b", 2), ("c", 2)):
        for i in range(n_state):
            outs.append(jnp.stack([s[i] for s in st[key][0]]))
            outs.append(jnp.stack([s[i] for s in st[key][1]]))
    return tuple(outs)
```

```python
import functools
import math

import jax
import jax.numpy as jnp
from jax import lax
from jax.experimental import pallas as pl
from jax.experimental.pallas import tpu as pltpu

F32 = jnp.float32
BF16 = jnp.bfloat16

D_MODEL = 1024
DEPTH = 4
PAGE_SIZE = 128
N_HEADS = 16
HEAD_DIM = D_MODEL // N_HEADS
N_MIXERS = 3
Q_BLOCK = 128

NSA_KV_HEADS = 2
NSA_CMP_BLOCK = 64
NSA_TOPK = 16
NSA_WINDOW = 512
NSA_FORCED_SCORE = float(N_HEADS + 1)
SB_KV_HEADS = 4
SWA_KV_HEADS = 2
SWA_WINDOW = 128

REL_BUCKETS = 32
REL_MAX_DIST = 128

MOE_GROUPS = 4
MOE_EXPERTS_PER_GROUP = 8
MOE_EXPERTS = MOE_GROUPS * MOE_EXPERTS_PER_GROUP
MOE_TOPK = 2
MOE_D_FF = 512

DEEPNORM_ALPHA = (2 * DEPTH) ** 0.25
LN_EPS = 1e-5
NEG_INF = -1e30

VMEM_LIMIT_BYTES = 56 * 1024 * 1024
LANES = 128
ROW_TILE = 256
MOE_TILE = 256
PAGES_PER_STEP = 16
POOL_PAGES_PER_STEP = 128

NSA_COLS = dict(kc=8, vc=9, ks=10, vs=11, kw=12, vw=13, gate=14)
NSA_NTOT = 15 * LANES
SB_NTOT = D_MODEL + 2 * SB_KV_HEADS * HEAD_DIM
SWA_COLS = dict(k=8, v=9)
SWA_NTOT = 10 * LANES


def _cparams(*sem):
    return pltpu.CompilerParams(dimension_semantics=sem, vmem_limit_bytes=VMEM_LIMIT_BYTES)


def _dot_nt(a, b):
    return lax.dot_general(a, b, (((1,), (1,)), ((), ())), preferred_element_type=F32)


def _mm_kernel(x_ref, w_ref, b_ref, o_ref, acc_ref, *, act, nk):
    k = pl.program_id(2)

    @pl.when(k == 0)
    def _():
        acc_ref[...] = jnp.zeros_like(acc_ref)

    x = x_ref[...].astype(F32)
    if act == "silu":
        x = x * jax.nn.sigmoid(x)
    acc_ref[...] += jnp.dot(x.astype(BF16), w_ref[...].astype(BF16), preferred_element_type=F32)

    @pl.when(k == nk - 1)
    def _():
        o_ref[...] = (acc_ref[...] + b_ref[...]).astype(o_ref.dtype)


def _mm(x, w, b=None, *, act=None, tm, tn, tk):
    m = x.shape[0]
    kdim, n = w.shape
    if b is None:
        b = jnp.zeros((n,), F32)
    b = b.reshape(1, n).astype(F32)
    nk = kdim // tk
    return pl.pallas_call(
        functools.partial(_mm_kernel, act=act, nk=nk),
        grid=(m // tm, n // tn, nk),
        in_specs=[
            pl.BlockSpec((tm, tk), lambda i, j, k: (i, k)),
            pl.BlockSpec((tk, tn), lambda i, j, k: (k, j)),
            pl.BlockSpec((1, tn), lambda i, j, k: (0, j)),
        ],
        out_specs=pl.BlockSpec((tm, tn), lambda i, j, k: (i, j)),
        out_shape=jax.ShapeDtypeStruct((m, n), F32),
        scratch_shapes=[pltpu.VMEM((tm, tn), F32)],
        compiler_params=_cparams("parallel", "parallel", "arbitrary"),
        name="mm",
    )(x, w, b)


def _modmm_kernel(x_ref, sh_ref, sc_ref, w_ref, b_ref, o_ref, x_ref2, *, second):
    h = x_ref[...] * (1.0 + sc_ref[0]) + sh_ref[0]
    y = jnp.dot(h.astype(BF16), w_ref[...], preferred_element_type=F32) + b_ref[...]
    o_ref[...] = y
    if second == "h":
        x_ref2[...] = h
    else:
        x_ref2[...] = y.astype(BF16)


def _mod_spec(mod, tm, rows_per_mod):
    r = mod.shape[1]
    if r == 1:
        tiles = rows_per_mod // tm
        return pl.BlockSpec((1, 1, D_MODEL), lambda i: (i // tiles, 0, 0))
    return pl.BlockSpec((1, tm, D_MODEL), lambda i: (i, 0, 0))


def _modmm(x, shift, scale, w_bf16, bias, rows_per_mod, *, second, tm=ROW_TILE):
    t = x.shape[0]
    n = w_bf16.shape[1]
    out_shape = [jax.ShapeDtypeStruct((t, n), F32)]
    out_specs = [pl.BlockSpec((tm, n), lambda i: (i, 0))]
    if second == "h":
        out_shape.append(jax.ShapeDtypeStruct((t, D_MODEL), F32))
        out_specs.append(pl.BlockSpec((tm, D_MODEL), lambda i: (i, 0)))
    else:
        out_shape.append(jax.ShapeDtypeStruct((t, n), BF16))
        out_specs.append(pl.BlockSpec((tm, n), lambda i: (i, 0)))
    return pl.pallas_call(
        functools.partial(_modmm_kernel, second=second),
        grid=(t // tm,),
        in_specs=[
            pl.BlockSpec((tm, D_MODEL), lambda i: (i, 0)),
            _mod_spec(shift, tm, rows_per_mod),
            _mod_spec(scale, tm, rows_per_mod),
            pl.BlockSpec((D_MODEL, n), lambda i: (0, 0)),
            pl.BlockSpec((1, n), lambda i: (0, 0)),
        ],
        out_specs=out_specs,
        out_shape=out_shape,
        compiler_params=_cparams("parallel"),
        name="modmm",
    )(x, shift, scale, w_bf16, bias.reshape(1, n).astype(F32))


def _layer_norm_rows(r, g, b):
    mu = jnp.mean(r, axis=-1, keepdims=True)
    d = r - mu
    var = jnp.mean(d * d, axis=-1, keepdims=True)
    return d * lax.rsqrt(var + LN_EPS) * g + b


def _proj_res_ln_kernel(a_ref, x_ref, gt_ref, w_ref, g_ref, b_ref, o_ref):
    y = jnp.dot(a_ref[...], w_ref[...], preferred_element_type=F32)
    r = DEEPNORM_ALPHA * x_ref[...] + (1.0 + gt_ref[0]) * y
    o_ref[...] = _layer_norm_rows(r, g_ref[...], b_ref[...])


def _proj_res_ln(a_bf16, x, gate, w_bf16, ln_g, ln_b, rows_per_mod, *, tm=ROW_TILE):
    t = x.shape[0]
    return pl.pallas_call(
        _proj_res_ln_kernel,
        grid=(t // tm,),
        in_specs=[
            pl.BlockSpec((tm, D_MODEL), lambda i: (i, 0)),
            pl.BlockSpec((tm, D_MODEL), lambda i: (i, 0)),
            _mod_spec(gate, tm, rows_per_mod),
            pl.BlockSpec((D_MODEL, D_MODEL), lambda i: (0, 0)),
            pl.BlockSpec((1, D_MODEL), lambda i: (0, 0)),
            pl.BlockSpec((1, D_MODEL), lambda i: (0, 0)),
        ],
        out_specs=pl.BlockSpec((tm, D_MODEL), lambda i: (i, 0)),
        out_shape=jax.ShapeDtypeStruct((t, D_MODEL), F32),
        compiler_params=_cparams("parallel"),
        name="proj_res_ln",
    )(a_bf16, x, gate, w_bf16, ln_g.reshape(1, -1), ln_b.reshape(1, -1))


def _combine_res_ln_kernel(f0_ref, f1_ref, w_ref, x_ref, gt_ref, g_ref, b_ref, o_ref):
    w = w_ref[...]
    f = f0_ref[...] * w[:, 0:1] + f1_ref[...] * w[:, 1:2]
    r = DEEPNORM_ALPHA * x_ref[...] + (1.0 + gt_ref[0]) * f
    o_ref[...] = _layer_norm_rows(r, g_ref[...], b_ref[...])


def _combine_res_ln(f0, f1, w, x, gate, ln_g, ln_b, rows_per_mod, *, tm=ROW_TILE):
    t = x.shape[0]
    row = lambda i: (i, 0)
    return pl.pallas_call(
        _combine_res_ln_kernel,
        grid=(t // tm,),
        in_specs=[
            pl.BlockSpec((tm, D_MODEL), row),
            pl.BlockSpec((tm, D_MODEL), row),
            pl.BlockSpec((tm, MOE_TOPK), row),
            pl.BlockSpec((tm, D_MODEL), row),
            _mod_spec(gate, tm, rows_per_mod),
            pl.BlockSpec((1, D_MODEL), lambda i: (0, 0)),
            pl.BlockSpec((1, D_MODEL), lambda i: (0, 0)),
        ],
        out_specs=pl.BlockSpec((tm, D_MODEL), row),
        out_shape=jax.ShapeDtypeStruct((t, D_MODEL), F32),
        compiler_params=_cparams("parallel"),
        name="combine_res_ln",
    )(f0, f1, w, x, gate, ln_g.reshape(1, -1), ln_b.reshape(1, -1))


def _moe_ffn_kernel(te_ref, tv_ref, x_ref, wg_ref, wu_ref, wd_ref, o_ref, wg_b, wu_b, wd_b):
    i = pl.program_id(0)
    e = te_ref[i]
    prev = te_ref[jnp.maximum(i - 1, 0)]

    @pl.when((i == 0) | (e != prev))
    def _():
        wg_b[...] = wg_ref[0].astype(BF16)
        wu_b[...] = wu_ref[0].astype(BF16)
        wd_b[...] = wd_ref[0].astype(BF16)

    @pl.when(tv_ref[i] > 0)
    def _():
        x = x_ref[...].astype(BF16)
        g = jnp.dot(x, wg_b[...], preferred_element_type=F32)
        u = jnp.dot(x, wu_b[...], preferred_element_type=F32)
        hmid = (g * jax.nn.sigmoid(g)) * u
        o_ref[...] = jnp.dot(hmid.astype(BF16), wd_b[...], preferred_element_type=F32)

    @pl.when(tv_ref[i] == 0)
    def _():
        o_ref[...] = jnp.zeros_like(o_ref)


def _moe_ffn(x_sorted, tile_e, tile_valid, w_gate, w_up, w_down, *, tm=MOE_TILE):
    p = x_sorted.shape[0]
    n_tiles = p // tm
    grid_spec = pltpu.PrefetchScalarGridSpec(
        num_scalar_prefetch=2,
        grid=(n_tiles,),
        in_specs=[
            pl.BlockSpec((tm, D_MODEL), lambda i, te, tv: (i, 0)),
            pl.BlockSpec((1, D_MODEL, MOE_D_FF), lambda i, te, tv: (te[i], 0, 0)),
            pl.BlockSpec((1, D_MODEL, MOE_D_FF), lambda i, te, tv: (te[i], 0, 0)),
            pl.BlockSpec((1, MOE_D_FF, D_MODEL), lambda i, te, tv: (te[i], 0, 0)),
        ],
        out_specs=pl.BlockSpec((tm, D_MODEL), lambda i, te, tv: (i, 0)),
        scratch_shapes=[
            pltpu.VMEM((D_MODEL, MOE_D_FF), BF16),
            pltpu.VMEM((D_MODEL, MOE_D_FF), BF16),
            pltpu.VMEM((MOE_D_FF, D_MODEL), BF16),
        ],
    )
    return pl.pallas_call(
        _moe_ffn_kernel,
        grid_spec=grid_spec,
        out_shape=jax.ShapeDtypeStruct((p, D_MODEL), F32),
        compiler_params=_cparams("arbitrary"),
        name="moe_ffn",
    )(tile_e, tile_valid, x_sorted, w_gate, w_up, w_down)


def _route(logits):
    g_logits = logits[:, :MOE_GROUPS]
    g_sel = jnp.argmax(g_logits, axis=-1)
    g_w = jnp.take_along_axis(jax.nn.softmax(g_logits, axis=-1), g_sel[:, None], axis=-1)
    e_logits = logits[:, MOE_GROUPS:MOE_GROUPS + MOE_EXPERTS].reshape(-1, MOE_GROUPS, MOE_EXPERTS_PER_GROUP)
    e_in = jnp.take_along_axis(e_logits, g_sel[:, None, None], axis=1)[:, 0]
    top_v, top_i = lax.top_k(e_in, MOE_TOPK)
    e_w = jax.nn.softmax(top_v, axis=-1) * g_w
    e_idx = (g_sel[:, None] * MOE_EXPERTS_PER_GROUP + top_i).astype(jnp.int32)
    return e_idx, e_w


def _moe_dispatch(e_idx, tm):
    t = e_idx.shape[0]
    a = t * MOE_TOPK
    flat_e = e_idx.reshape(-1)
    order = jnp.argsort(flat_e)
    sorted_e = flat_e[order]
    counts = jnp.bincount(flat_e, length=MOE_EXPERTS)
    padded = (counts + tm - 1) // tm * tm
    pad_end = jnp.cumsum(padded)
    dest = (pad_end - padded)[sorted_e] + jnp.arange(a) - (jnp.cumsum(counts) - counts)[sorted_e]
    n_tiles = (a + MOE_EXPERTS * (tm - 1)) // tm + 1
    src = jnp.full((n_tiles * tm,), t, jnp.int32).at[dest].set((order // MOE_TOPK).astype(jnp.int32))
    tile_start = jnp.arange(n_tiles) * tm
    tile_e = jnp.minimum(jnp.sum(pad_end[None, :] <= tile_start[:, None], axis=1), MOE_EXPERTS - 1).astype(jnp.int32)
    tile_valid = (tile_start < pad_end[-1]).astype(jnp.int32)
    dest_by_assign = jnp.zeros((a,), jnp.int32).at[order].set(dest.astype(jnp.int32)).reshape(t, MOE_TOPK)
    return src, tile_e, tile_valid, dest_by_assign


def _rel_bucket(dist):
    n = jnp.maximum(dist, 0)
    exact = REL_BUCKETS // 2
    logv = jnp.log(jnp.maximum(n, 1).astype(F32) / exact) / math.log(REL_MAX_DIST / exact)
    large = jnp.minimum(exact + (logv * (REL_BUCKETS - exact)).astype(jnp.int32), REL_BUCKETS - 1)
    return jnp.where(n < exact, n, large)


def _bias_heads(rel_bias, dist, ok, shift=None):
    onehot = jax.nn.one_hot(_rel_bucket(dist), REL_BUCKETS, dtype=F32)
    b = jnp.transpose(jnp.dot(onehot, rel_bias.astype(F32), precision=lax.Precision.HIGHEST), (2, 0, 1))
    if shift is not None:
        b = b - shift[:, None, None]
    return jnp.where(ok[None], b, NEG_INF)


def _bias_rows(rel_bias, dist, ok, q_len):
    return _bias_heads(rel_bias, dist, ok).reshape(N_HEADS * q_len, dist.shape[1])


def _band_tables(rel_bias, n_delta, window, shift=None):
    qi = jnp.arange(Q_BLOCK)[:, None]
    ki = jnp.arange(Q_BLOCK)[None, :]
    tabs = []
    for delta in range(n_delta):
        d = delta * Q_BLOCK + qi - ki
        tabs.append(_bias_heads(rel_bias, d, (d >= 0) & (d <= window), shift))
    return jnp.stack(tabs)


def _build_qbd(q, n_groups):
    r = q.shape[0]
    hpg = N_HEADS // n_groups
    blocks = []
    for hh in range(N_HEADS):
        g = hh // hpg
        parts = []
        if g > 0:
            parts.append(jnp.zeros((r, g * HEAD_DIM), F32))
        parts.append(q[:, hh * HEAD_DIM:(hh + 1) * HEAD_DIM])
        if g < n_groups - 1:
            parts.append(jnp.zeros((r, (n_groups - 1 - g) * HEAD_DIM), F32))
        blocks.append(jnp.concatenate(parts, axis=1))
    return (jnp.concatenate(blocks, axis=0) * (HEAD_DIM ** -0.5)).astype(BF16)


def _aug_values(v_t):
    lane = lax.broadcasted_iota(jnp.int32, v_t.shape, 1)
    one = jnp.ones(v_t.shape, v_t.dtype)
    return [jnp.where(lane < HEAD_DIM, v_t, one), jnp.where(lane >= HEAD_DIM, v_t, one)]


def _flash_update(s, v_t, m_ref, acc_ref):
    m_prev = m_ref[...]
    m_new = jnp.maximum(m_prev, jnp.max(s, axis=1, keepdims=True))
    alpha = jnp.exp(m_prev - m_new)
    n_rep = s.shape[1] // LANES
    p = jnp.exp(s - (m_new if n_rep == 1 else jnp.concatenate([m_new] * n_rep, axis=1))).astype(BF16)
    v_aug = _aug_values(v_t)
    half = p.shape[0] // 2
    pv = jnp.concatenate([jnp.dot(p[:half], v_aug[0], preferred_element_type=F32),
                          jnp.dot(p[half:], v_aug[1], preferred_element_type=F32)], axis=0)
    acc_ref[...] = alpha * acc_ref[...] + pv
    m_ref[...] = m_new


def _flash_result(acc_h):
    return acc_h / pltpu.roll(acc_h, HEAD_DIM, axis=1)


def _rank_select(score, cand, b_idx, n_blocks, k):
    rank = jnp.zeros(score.shape, F32)
    for bp in range(n_blocks):
        col = score[..., bp:bp + 1]
        tie = jnp.where(b_idx > bp, 1.0, 0.0)
        rank = rank + jnp.where(col > score, 1.0, jnp.where(col == score, tie, 0.0))
    return jnp.where(cand, jnp.where(rank < float(k), 1.0, 0.0), 0.0)


def _heads_to_lanes(read_rows, n_groups, r):
    hpg = N_HEADS // n_groups
    outs = []
    for hh in range(N_HEADS):
        g = hh // hpg
        outs.append(read_rows(hh)[:, g * HEAD_DIM:(g + 1) * HEAD_DIM])
    return jnp.concatenate(outs, axis=1)


def _nsa_prompt_kernel(q_ref, kcs_ref, vcs_ref, ks_ref, vs_ref, kw_ref, vw_ref, gl_ref, bg_ref,
                       tab_ref, bc_ref, ex_ref, o_ref,
                       m_ref, acc_ref, oc_ref, os_ref, sel_ref, *, nb):
    j = pl.program_id(1)
    g_n = NSA_KV_HEADS
    hpg = N_HEADS // g_n
    rows = N_HEADS * Q_BLOCK
    qbd = _build_qbd(q_ref[...].astype(F32), g_n)

    s_c = _dot_nt(qbd, kcs_ref[...].astype(BF16)) + bc_ref[0]
    qi = lax.broadcasted_iota(jnp.int32, (Q_BLOCK, nb), 0)
    bi = lax.broadcasted_iota(jnp.int32, (Q_BLOCK, nb), 1)
    t = j * Q_BLOCK + qi
    valid_c = (t - (bi + 1) * NSA_CMP_BLOCK + 1) >= 0
    e_c = jnp.exp(s_c - jnp.max(s_c, axis=1, keepdims=True))
    p_c = e_c / jnp.sum(e_c, axis=1, keepdims=True)
    p_c4 = jnp.where(valid_c[None, None], p_c.reshape(g_n, hpg, Q_BLOCK, nb), 0.0)
    oc_ref[...] = jnp.dot(p_c4.reshape(rows, nb).astype(BF16), vcs_ref[...].astype(BF16),
                          preferred_element_type=F32)

    imp = jnp.sum(p_c4, axis=1)
    cur = t // NSA_CMP_BLOCK
    forced = (bi == cur) | (bi == cur - 1) | (bi == 0)
    cand = bi <= cur
    for g in range(g_n):
        score = jnp.where(cand, jnp.where(forced, NSA_FORCED_SCORE, imp[g]), -1.0)
        sel_ref[g] = _rank_select(score, cand, bi, nb, min(NSA_TOPK, nb)).astype(BF16)

    def reset():
        m_ref[...] = jnp.full(m_ref.shape, NEG_INF, F32)
        acc_ref[...] = jnp.zeros(acc_ref.shape, F32)

    def masked(s, e_t):
        w = e_t.shape[1]
        msk = jnp.stack([jnp.dot(sel_ref[g], e_t, preferred_element_type=F32) for g in range(g_n)])
        madd = (1.0 - msk) * NEG_INF
        return (s.reshape(g_n, hpg, Q_BLOCK, w) + madd[:, None]).reshape(rows, w)

    reset()

    def sel_far(c, carry):
        off = pl.multiple_of(c * 2 * Q_BLOCK, 2 * Q_BLOCK)
        e_t = jnp.concatenate([ex_ref[2 * c], ex_ref[2 * c + 1]], axis=1)
        s = masked(_dot_nt(qbd, ks_ref[pl.ds(off, 2 * Q_BLOCK), :]), e_t)
        _flash_update(s, vs_ref[pl.ds(off, 2 * Q_BLOCK), :], m_ref, acc_ref)
        return carry

    n_pair = jnp.maximum(j - 1, 0) // 2
    lax.fori_loop(0, n_pair, sel_far, 0)

    def sel_near(kt, carry):
        off = pl.multiple_of(kt * Q_BLOCK, Q_BLOCK)
        s = _dot_nt(qbd, ks_ref[pl.ds(off, Q_BLOCK), :]) + tab_ref[jnp.minimum(j - kt, 2)]
        _flash_update(masked(s, ex_ref[kt]), vs_ref[pl.ds(off, Q_BLOCK), :], m_ref, acc_ref)
        return carry

    lax.fori_loop(2 * n_pair, j + 1, sel_near, 0)
    os_ref[...] = acc_ref[...]

    reset()

    def win_tile(i, carry):
        off = pl.multiple_of((j - i) * Q_BLOCK, Q_BLOCK)
        s = _dot_nt(qbd, kw_ref[pl.ds(off, Q_BLOCK), :]) + tab_ref[i]
        _flash_update(s, vw_ref[pl.ds(off, Q_BLOCK), :], m_ref, acc_ref)
        return carry

    lax.fori_loop(0, jnp.minimum(j, NSA_WINDOW // Q_BLOCK) + 1, win_tile, 0)

    sig = jax.nn.sigmoid(gl_ref[...] + bg_ref[...])

    def head_out(hh):
        r0 = hh * Q_BLOCK
        return (sig[:, 3 * hh:3 * hh + 1] * oc_ref[r0:r0 + Q_BLOCK, :]
                + sig[:, 3 * hh + 1:3 * hh + 2] * _flash_result(os_ref[r0:r0 + Q_BLOCK, :])
                + sig[:, 3 * hh + 2:3 * hh + 3] * _flash_result(acc_ref[r0:r0 + Q_BLOCK, :]))

    o_ref[...] = _heads_to_lanes(head_out, g_n, Q_BLOCK).astype(o_ref.dtype)


def _nsa_prompt_attn(y, yb, kcs, vcs, b_gate_row, tabs, bc, ex, n_batch, seq):
    nq = seq // Q_BLOCK
    nb = seq // NSA_CMP_BLOCK
    rows = N_HEADS * Q_BLOCK
    c = NSA_COLS
    kv = lambda col: pl.BlockSpec((seq, LANES), lambda n, j: (n, col))
    const = lambda shape: pl.BlockSpec(shape, lambda n, j: (0,) * len(shape))
    stacked = pltpu.VMEM((rows, LANES), F32)
    return pl.pallas_call(
        functools.partial(_nsa_prompt_kernel, nb=nb),
        grid=(n_batch, nq),
        in_specs=[
            pl.BlockSpec((Q_BLOCK, D_MODEL), lambda n, j: (n * nq + j, 0)),
            pl.BlockSpec((nb, LANES), lambda n, j: (n, 0)),
            pl.BlockSpec((nb, LANES), lambda n, j: (n, 0)),
            kv(c["ks"]), kv(c["vs"]), kv(c["kw"]), kv(c["vw"]),
            pl.BlockSpec((Q_BLOCK, LANES), lambda n, j: (n * nq + j, c["gate"])),
            const((1, LANES)),
            const(tabs.shape),
            pl.BlockSpec((1, rows, nb), lambda n, j: (j, 0, 0)),
            const(ex.shape),
        ],
        out_specs=pl.BlockSpec((Q_BLOCK, D_MODEL), lambda n, j: (n * nq + j, 0)),
        out_shape=jax.ShapeDtypeStruct((n_batch * seq, D_MODEL), BF16),
        scratch_shapes=[stacked, stacked, stacked, stacked, pltpu.VMEM((NSA_KV_HEADS, Q_BLOCK, nb), BF16)],
        compiler_params=_cparams("parallel", "arbitrary"),
        name="nsa_prompt",
    )(yb, kcs, vcs, yb, yb, yb, yb, y, b_gate_row, tabs, bc, ex)


def _suffix_sums(u, tri):
    u_hi = u.astype(BF16)
    u_lo = (u - u_hi.astype(F32)).astype(BF16)
    return (jnp.dot(u_hi, tri, preferred_element_type=F32) + jnp.dot(u_lo, tri, preferred_element_type=F32))


def _stick_break_tile(z, v_t, tri, c_prev, valid):
    sp = jnp.log(1.0 + jnp.exp(-jnp.abs(z)))
    ls = jnp.minimum(z, 0.0) - sp
    u = -jnp.maximum(z, 0.0) - sp
    if valid is not None:
        u = jnp.where(valid, u, 0.0)
    a = jnp.exp(ls + _suffix_sums(u, tri) + c_prev)
    if valid is not None:
        a = jnp.where(valid, a, 0.0)
    return (jnp.dot(a.astype(BF16), v_t, preferred_element_type=F32),
            c_prev + jnp.sum(u, axis=1, keepdims=True))


def _sb_prompt_kernel(q_ref, k_ref, v_ref, tri_ref, o_ref, acc_ref, c_ref):
    j = pl.program_id(1)
    g_n = SB_KV_HEADS
    rows = N_HEADS * Q_BLOCK
    qbd = _build_qbd(q_ref[...].astype(F32), g_n)
    acc_ref[...] = jnp.zeros(acc_ref.shape, F32)
    c_ref[...] = jnp.zeros(c_ref.shape, F32)

    def tile(kt, c_prev, valid):
        off = 0 if isinstance(kt, int) else pl.multiple_of(kt * Q_BLOCK, Q_BLOCK)
        z = _dot_nt(qbd, k_ref[pl.ds(off, Q_BLOCK), :])
        return _stick_break_tile(z, v_ref[pl.ds(off, Q_BLOCK), :], tri_ref[...], c_prev, valid)

    qi = lax.broadcasted_iota(jnp.int32, (rows, Q_BLOCK), 0) % Q_BLOCK
    ki = lax.broadcasted_iota(jnp.int32, (rows, Q_BLOCK), 1)
    av, c_new = tile(j, c_ref[...], ki < qi)
    acc_ref[...] = av
    c_ref[...] = c_new

    def pair(i, carry):
        av_a, c_a = tile(j - 1 - 2 * i, c_ref[...], None)
        av_b, c_b = tile(j - 2 - 2 * i, c_a, None)
        acc_ref[...] += av_a + av_b
        c_ref[...] = c_b
        return carry

    lax.fori_loop(0, j // 2, pair, 0)

    @pl.when(j % 2 == 1)
    def _():
        av_l, _ = tile(0, c_ref[...], None)
        acc_ref[...] += av_l
    o_ref[...] = _heads_to_lanes(lambda hh: acc_ref[hh * Q_BLOCK:(hh + 1) * Q_BLOCK, :], g_n,
                                 Q_BLOCK).astype(o_ref.dtype)


def _tri_strict_lower():
    qi = jnp.arange(Q_BLOCK)[:, None]
    ki = jnp.arange(Q_BLOCK)[None, :]
    return (qi > ki).astype(BF16)


def _sb_prompt_attn(yb, n_batch, seq):
    nq = seq // Q_BLOCK
    kvw = SB_KV_HEADS * HEAD_DIM
    rows = N_HEADS * Q_BLOCK
    return pl.pallas_call(
        _sb_prompt_kernel,
        grid=(n_batch, nq),
        in_specs=[
            pl.BlockSpec((Q_BLOCK, D_MODEL), lambda n, j: (n * nq + j, 0)),
            pl.BlockSpec((seq, kvw), lambda n, j: (n, D_MODEL // kvw)),
            pl.BlockSpec((seq, kvw), lambda n, j: (n, D_MODEL // kvw + 1)),
            pl.BlockSpec((Q_BLOCK, Q_BLOCK), lambda n, j: (0, 0)),
        ],
        out_specs=pl.BlockSpec((Q_BLOCK, D_MODEL), lambda n, j: (n * nq + j, 0)),
        out_shape=jax.ShapeDtypeStruct((n_batch * seq, D_MODEL), BF16),
        scratch_shapes=[pltpu.VMEM((rows, kvw), F32), pltpu.VMEM((rows, Q_BLOCK), F32)],
        compiler_params=_cparams("parallel", "arbitrary"),
        name="sb_prompt",
    )(yb, yb, yb, _tri_strict_lower())


def _swa_prompt_kernel(q_ref, k_ref, v_ref, tab_ref, sink_ref, o_ref, m_ref, acc_ref):
    j = pl.program_id(1)
    g_n = SWA_KV_HEADS
    rows = N_HEADS * Q_BLOCK
    qbd = _build_qbd(q_ref[...].astype(F32), g_n)
    m_ref[...] = sink_ref[...]
    lane = lax.broadcasted_iota(jnp.int32, (rows, LANES), 1)
    row = lax.broadcasted_iota(jnp.int32, (rows, LANES), 0)
    own = (lane >= HEAD_DIM) == (row >= rows // g_n)
    acc_ref[...] = jnp.where(own, 0.0, 1.0)

    def body(i, carry):
        off = pl.multiple_of((j - i) * Q_BLOCK, Q_BLOCK)
        s = _dot_nt(qbd, k_ref[pl.ds(off, Q_BLOCK), :]) + tab_ref[i]
        _flash_update(s, v_ref[pl.ds(off, Q_BLOCK), :], m_ref, acc_ref)
        return carry

    lax.fori_loop(0, jnp.minimum(j, SWA_WINDOW // Q_BLOCK) + 1, body, 0)
    o_ref[...] = _heads_to_lanes(lambda hh: _flash_result(acc_ref[hh * Q_BLOCK:(hh + 1) * Q_BLOCK, :]), g_n,
                                 Q_BLOCK).astype(o_ref.dtype)


def _swa_prompt_attn(yb, tabs, sink_rows, n_batch, seq):
    nq = seq // Q_BLOCK
    rows = N_HEADS * Q_BLOCK
    c = SWA_COLS
    return pl.pallas_call(
        _swa_prompt_kernel,
        grid=(n_batch, nq),
        in_specs=[
            pl.BlockSpec((Q_BLOCK, D_MODEL), lambda n, j: (n * nq + j, 0)),
            pl.BlockSpec((seq, LANES), lambda n, j: (n, c["k"])),
            pl.BlockSpec((seq, LANES), lambda n, j: (n, c["v"])),
            pl.BlockSpec(tabs.shape, lambda n, j: (0, 0, 0)),
            pl.BlockSpec((rows, LANES), lambda n, j: (0, 0)),
        ],
        out_specs=pl.BlockSpec((Q_BLOCK, D_MODEL), lambda n, j: (n * nq + j, 0)),
        out_shape=jax.ShapeDtypeStruct((n_batch * seq, D_MODEL), BF16),
        scratch_shapes=[pltpu.VMEM((rows, LANES), F32), pltpu.VMEM((rows, LANES), F32)],
        compiler_params=_cparams("parallel", "arbitrary"),
        name="swa_prompt",
    )(yb, yb, yb, tabs, sink_rows)


def _pages_t(pool):
    l, n_pool, page, g, hd = pool.shape
    return jnp.transpose(pool, (0, 1, 3, 4, 2)).reshape(l * n_pool, g * hd, page)


def _buf_t(buf):
    n, w, g, hd = buf.shape
    return jnp.transpose(buf, (0, 2, 3, 1)).reshape(n, g * hd, w)


def _pad_rows(x, n):
    return jnp.concatenate([x, jnp.zeros((n - x.shape[0], x.shape[1]), x.dtype)], axis=0)


def _pool_compress_kernel(x_ref, w_ref, o_ref, *, n_pages, g_n):
    kvw = g_n * HEAD_DIM
    accs = [jnp.zeros((n_pages, LANES), F32) for _ in range(g_n)]
    for r in range(kvw):
        x_r = x_ref[pl.ds(r, n_pages, stride=kvw), :].astype(BF16)
        accs[r // HEAD_DIM] = accs[r // HEAD_DIM] + jnp.dot(x_r, w_ref[r], preferred_element_type=F32)
    o_ref[...] = jnp.concatenate(accs, axis=1)


def _pool_compress(pool_t2d, w_rows, slot, n_pool, g_n):
    kvw = g_n * HEAD_DIM
    n_pages = next(c for c in (POOL_PAGES_PER_STEP, 64, 32, 16, 8) if n_pool % c == 0)
    steps = n_pool // n_pages
    return pl.pallas_call(
        functools.partial(_pool_compress_kernel, n_pages=n_pages, g_n=g_n),
        grid=(steps,),
        in_specs=[
            pl.BlockSpec((n_pages * kvw, PAGE_SIZE), lambda i: (i + slot * steps, 0)),
            pl.BlockSpec(w_rows.shape, lambda i: (0, 0, 0)),
        ],
        out_specs=pl.BlockSpec((n_pages, g_n * LANES), lambda i: (i, 0)),
        out_shape=jax.ShapeDtypeStruct((n_pool, g_n * LANES), F32),
        compiler_params=_cparams("parallel"),
        name="pool_compress",
    )(pool_t2d, w_rows)


def _nsa_sample_kernel(pt_ref, q_ref, ksn_ref, vsn_ref, kwn_ref, vwn_ref, gl_ref, bg_ref,
                       kcs_ref, vcs_ref, bcs_ref, tlast_ref, tfar_ref, tnew_ref, wtb_ref, ex_ref,
                       kwb_ref, vwb_ref, *rest, ppc, nchunk, nb, t_new):
    k_pages = rest[:ppc]
    v_pages = rest[ppc:2 * ppc]
    o_ref = rest[2 * ppc]
    m_ref, l_ref, acc_ref, oc_ref, ow_ref, sel_ref, qbd_ref = rest[2 * ppc + 1:]
    c = pl.program_id(1)
    g_n = NSA_KV_HEADS
    hpg = N_HEADS // g_n
    rows = N_HEADS * t_new
    nbp = kcs_ref.shape[0]

    @pl.when(c == 0)
    def _():
        qbd = _build_qbd(q_ref[...], g_n)
        qbd_ref[...] = qbd
        s_c = _dot_nt(qbd, kcs_ref[...].astype(BF16)) + bcs_ref[...]
        bi = lax.broadcasted_iota(jnp.int32, (rows, nbp), 1)
        e_c = jnp.exp(s_c - jnp.max(s_c, axis=1, keepdims=True))
        p_c = jnp.where(bi < nb - 1, e_c / jnp.sum(e_c, axis=1, keepdims=True), 0.0)
        oc_ref[...] = jnp.dot(p_c.astype(BF16), vcs_ref[...].astype(BF16), preferred_element_type=F32)
        imp = jnp.sum(p_c.reshape(g_n, hpg, t_new, nbp), axis=1)
        b3 = lax.broadcasted_iota(jnp.int32, (g_n, t_new, nbp), 2)
        forced = (b3 == nb - 1) | (b3 == nb - 2) | (b3 == 0)
        cand = b3 <= nb - 1
        score = jnp.where(cand, jnp.where(forced, NSA_FORCED_SCORE, imp), -1.0)
        sel = _rank_select(score, cand, b3, nb, min(NSA_TOPK, nb))
        sel_ref[...] = jnp.broadcast_to(sel[:, None], (g_n, hpg, t_new, nbp)).reshape(rows, nbp).astype(BF16)

        s_b = jnp.dot(qbd, kwb_ref[...].astype(BF16), preferred_element_type=F32) + wtb_ref[...]
        s_n = _dot_nt(qbd, _pad_rows(kwn_ref[...], LANES).astype(BF16)) + tnew_ref[...]
        m_w = jnp.maximum(jnp.max(s_b, axis=1, keepdims=True), jnp.max(s_n, axis=1, keepdims=True))
        p_b = jnp.exp(s_b - m_w)
        p_n = jnp.exp(s_n - m_w)
        l_w = jnp.sum(p_b, axis=1, keepdims=True) + jnp.sum(p_n, axis=1, keepdims=True)
        o_w = (_dot_nt(p_b.astype(BF16), vwb_ref[...].astype(BF16))
               + jnp.dot(p_n.astype(BF16), _pad_rows(vwn_ref[...], LANES).astype(BF16),
                         preferred_element_type=F32))
        ow_ref[...] = o_w / l_w

        s0 = _dot_nt(qbd, _pad_rows(ksn_ref[...], LANES).astype(BF16)) + tnew_ref[...]
        m0 = jnp.max(s0, axis=1, keepdims=True)
        p0 = jnp.exp(s0 - m0)
        m_ref[...] = jnp.broadcast_to(m0, m_ref.shape)
        l_ref[...] = jnp.broadcast_to(jnp.sum(p0, axis=1, keepdims=True), l_ref.shape)
        acc_ref[...] = jnp.dot(p0.astype(BF16), _pad_rows(vsn_ref[...], LANES).astype(BF16),
                               preferred_element_type=F32)

    qbd = qbd_ref[...]
    msk = jnp.dot(sel_ref[...], ex_ref[...], preferred_element_type=F32)
    parts = []
    for r in range(ppc):
        s_r = jnp.dot(qbd, k_pages[r][...].astype(BF16), preferred_element_type=F32)
        if r == ppc - 1:
            s_r = s_r + jnp.where(c == nchunk - 1, tlast_ref[...], tfar_ref[...])
        else:
            s_r = s_r + tfar_ref[...]
        parts.append(s_r)
    s = jnp.concatenate(parts, axis=1) + (1.0 - msk) * NEG_INF
    m_prev = m_ref[...]
    m_new = jnp.maximum(m_prev, jnp.max(s, axis=1, keepdims=True))
    alpha = jnp.exp(m_prev - m_new)
    p = jnp.exp(s - m_new[:, 0:1])
    pv = _dot_nt(p[:, 0:PAGE_SIZE].astype(BF16), v_pages[0][...].astype(BF16))
    for r in range(1, ppc):
        pv = pv + _dot_nt(p[:, r * PAGE_SIZE:(r + 1) * PAGE_SIZE].astype(BF16), v_pages[r][...].astype(BF16))
    l_ref[...] = alpha * l_ref[...] + jnp.sum(p, axis=1, keepdims=True)
    acc_ref[...] = alpha * acc_ref[...] + pv
    m_ref[...] = m_new

    @pl.when(c == nchunk - 1)
    def _():
        sig = jax.nn.sigmoid(gl_ref[...] + bg_ref[...])

        def head_out(hh):
            r0 = hh * t_new
            o_s = acc_ref[r0:r0 + t_new, :] / l_ref[r0:r0 + t_new, :]
            return (sig[:, 3 * hh:3 * hh + 1] * oc_ref[r0:r0 + t_new, :]
                    + sig[:, 3 * hh + 1:3 * hh + 2] * o_s
                    + sig[:, 3 * hh + 2:3 * hh + 3] * ow_ref[r0:r0 + t_new, :])

        o_ref[...] = _heads_to_lanes(head_out, g_n, t_new).astype(o_ref.dtype)


def _nsa_sample_tables(rel_bias, past, t_new, wb, nb, nbp):
    i = jnp.arange(t_new)[:, None]
    ki = jnp.arange(LANES)[None, :]
    d_new = i - ki
    tnew = _bias_rows(rel_bias, d_new, (d_new >= 0) & (ki < t_new), t_new)
    d_last = PAGE_SIZE + i - ki
    tlast = _bias_rows(rel_bias, d_last, d_last >= 0, t_new)
    d_far = jnp.full((t_new, LANES), 2 * PAGE_SIZE)
    tfar = _bias_rows(rel_bias, d_far, d_far >= 0, t_new)
    r = jnp.arange(wb)[None, :]
    d_w = wb + i - r
    wtb = _bias_rows(rel_bias, d_w, (d_w <= NSA_WINDOW) & (past - wb + r >= 0), t_new)
    b = jnp.arange(nbp)[None, :]
    d_c = past + i - ((b + 1) * NSA_CMP_BLOCK - 1)
    bcs = _bias_rows(rel_bias, d_c, (d_c >= 0) & (b < nb), t_new)
    return tnew, tlast, tfar, wtb, bcs


def _nsa_sample_attn(y, kcs, vcs, b_gate_row, tables, ex, kwb_t, vwb_t, pool_k_t, pool_v_t, pt_flat,
                     n_seq, t_new, n_pages, nb):
    ppc = PAGES_PER_STEP
    nchunk = n_pages // ppc
    rows = N_HEADS * t_new
    nbp = kcs.shape[1]
    wb = kwb_t.shape[2]
    tnew, tlast, tfar, wtb, bcs = tables
    c = NSA_COLS
    new = lambda col: pl.BlockSpec((t_new, LANES), lambda n, ch, pt: (n, col))
    const = lambda shape: pl.BlockSpec(shape, lambda n, ch, pt: (0,) * len(shape))

    def page_spec(r):
        return pl.BlockSpec((None, LANES, PAGE_SIZE),
                            lambda n, ch, pt: (pt[n * n_pages + ch * ppc + r], 0, 0))

    in_specs = [
        pl.BlockSpec((t_new, D_MODEL), lambda n, ch, pt: (n, 0)),
        new(c["ks"]), new(c["vs"]), new(c["kw"]), new(c["vw"]), new(c["gate"]),
        const((1, LANES)),
        pl.BlockSpec((None, nbp, LANES), lambda n, ch, pt: (n, 0, 0)),
        pl.BlockSpec((None, nbp, LANES), lambda n, ch, pt: (n, 0, 0)),
        const((rows, nbp)), const((rows, LANES)), const((rows, LANES)), const((rows, LANES)),
        const((rows, wb)),
        pl.BlockSpec((nbp, ppc * PAGE_SIZE), lambda n, ch, pt: (0, ch)),
        pl.BlockSpec((None, LANES, wb), lambda n, ch, pt: (n, 0, 0)),
        pl.BlockSpec((None, LANES, wb), lambda n, ch, pt: (n, 0, 0)),
    ] + [page_spec(r) for r in range(ppc)] + [page_spec(r) for r in range(ppc)]
    grid_spec = pltpu.PrefetchScalarGridSpec(
        num_scalar_prefetch=1,
        grid=(n_seq, nchunk),
        in_specs=in_specs,
        out_specs=pl.BlockSpec((t_new, D_MODEL), lambda n, ch, pt: (n, 0)),
        scratch_shapes=[pltpu.VMEM((rows, LANES), F32)] * 5 + [pltpu.VMEM((rows, nbp), BF16),
                                                               pltpu.VMEM((rows, LANES), BF16)],
    )
    return pl.pallas_call(
        functools.partial(_nsa_sample_kernel, ppc=ppc, nchunk=nchunk, nb=nb, t_new=t_new),
        grid_spec=grid_spec,
        out_shape=jax.ShapeDtypeStruct((n_seq * t_new, D_MODEL), BF16),
        compiler_params=_cparams("parallel", "arbitrary"),
        name="nsa_sample",
    )(pt_flat, y, y, y, y, y, y, b_gate_row, kcs, vcs, bcs, tlast, tfar, tnew, wtb, ex, kwb_t, vwb_t,
      *([pool_k_t] * ppc), *([pool_v_t] * ppc))


def _sb_sample_kernel(pt_ref, q_ref, kn_ref, vn_ref, tri_ref, *rest, ppc, t_new):
    k_pages = rest[:ppc]
    v_pages = rest[ppc:2 * ppc]
    o_ref = rest[2 * ppc]
    qbd_ref, acc_ref, c_ref = rest[2 * ppc + 1:]
    c = pl.program_id(1)
    g_n = SB_KV_HEADS
    rows = N_HEADS * t_new

    @pl.when(c == 0)
    def _():
        qbd0 = _build_qbd(q_ref[...], g_n)
        qbd_ref[...] = qbd0
        ki = lax.broadcasted_iota(jnp.int32, (rows, PAGE_SIZE), 1)
        qi = lax.broadcasted_iota(jnp.int32, (rows, PAGE_SIZE), 0) % t_new
        z = _dot_nt(qbd0, _pad_rows(kn_ref[...], PAGE_SIZE).astype(BF16))
        av, c_new = _stick_break_tile(z, _pad_rows(vn_ref[...], PAGE_SIZE).astype(BF16), tri_ref[...],
                                      jnp.zeros((rows, PAGE_SIZE), F32), ki < qi)
        acc_ref[...] = av
        c_ref[...] = c_new

    qbd = qbd_ref[...]
    tri = tri_ref[...]
    c_run = c_ref[...]
    acc = acc_ref[...]
    for r in reversed(range(ppc)):
        z = jnp.dot(qbd, k_pages[r][...].astype(BF16), preferred_element_type=F32)
        sp = jnp.log(1.0 + jnp.exp(-jnp.abs(z)))
        ls = jnp.minimum(z, 0.0) - sp
        u = -jnp.maximum(z, 0.0) - sp
        a = jnp.exp(ls + _suffix_sums(u, tri) + c_run)
        acc = acc + _dot_nt(a.astype(BF16), v_pages[r][...].astype(BF16))
        c_run = c_run + jnp.sum(u, axis=1, keepdims=True)
    acc_ref[...] = acc
    c_ref[...] = c_run

    @pl.when(c == pl.num_programs(1) - 1)
    def _():
        o_ref[...] = _heads_to_lanes(lambda hh: acc_ref[hh * t_new:(hh + 1) * t_new, :], g_n,
                                     t_new).astype(o_ref.dtype)


def _sb_sample_attn(y, pool_k_t, pool_v_t, pt_flat, n_seq, t_new, n_pages):
    ppc = PAGES_PER_STEP
    nchunk = n_pages // ppc
    rows = N_HEADS * t_new
    kvw = SB_KV_HEADS * HEAD_DIM

    def page_spec(r):
        return pl.BlockSpec((None, kvw, PAGE_SIZE),
                            lambda n, ch, pt: (pt[n * n_pages + (nchunk - 1 - ch) * ppc + r], 0, 0))

    in_specs = [
        pl.BlockSpec((t_new, D_MODEL), lambda n, ch, pt: (n, 0)),
        pl.BlockSpec((t_new, kvw), lambda n, ch, pt: (n, D_MODEL // kvw)),
        pl.BlockSpec((t_new, kvw), lambda n, ch, pt: (n, D_MODEL // kvw + 1)),
        pl.BlockSpec((PAGE_SIZE, PAGE_SIZE), lambda n, ch, pt: (0, 0)),
    ] + [page_spec(r) for r in range(ppc)] + [page_spec(r) for r in range(ppc)]
    grid_spec = pltpu.PrefetchScalarGridSpec(
        num_scalar_prefetch=1,
        grid=(n_seq, nchunk),
        in_specs=in_specs,
        out_specs=pl.BlockSpec((t_new, D_MODEL), lambda n, ch, pt: (n, 0)),
        scratch_shapes=[pltpu.VMEM((rows, kvw), BF16), pltpu.VMEM((rows, kvw), F32),
                        pltpu.VMEM((rows, PAGE_SIZE), F32)],
    )
    return pl.pallas_call(
        functools.partial(_sb_sample_kernel, ppc=ppc, t_new=t_new),
        grid_spec=grid_spec,
        out_shape=jax.ShapeDtypeStruct((n_seq * t_new, D_MODEL), BF16),
        compiler_params=_cparams("parallel", "arbitrary"),
        name="sb_sample",
    )(pt_flat, y, y, y, _tri_strict_lower(), *([pool_k_t] * ppc), *([pool_v_t] * ppc))


def _swa_sample_kernel(q_ref, kn_ref, vn_ref, kb_ref, vb_ref, tb_ref, tn_ref, sink_ref, o_ref, *, t_new):
    qbd = _build_qbd(q_ref[...], SWA_KV_HEADS)
    s_b = jnp.dot(qbd, kb_ref[...].astype(BF16), preferred_element_type=F32) + tb_ref[...]
    s_n = _dot_nt(qbd, _pad_rows(kn_ref[...], LANES).astype(BF16)) + tn_ref[...]
    sink = sink_ref[...][:, 0:1]
    m = jnp.maximum(jnp.maximum(jnp.max(s_b, axis=1, keepdims=True), jnp.max(s_n, axis=1, keepdims=True)), sink)
    p_b = jnp.exp(s_b - m)
    p_n = jnp.exp(s_n - m)
    l = jnp.sum(p_b, axis=1, keepdims=True) + jnp.sum(p_n, axis=1, keepdims=True) + jnp.exp(sink - m)
    o = (_dot_nt(p_b.astype(BF16), vb_ref[...].astype(BF16))
         + jnp.dot(p_n.astype(BF16), _pad_rows(vn_ref[...], LANES).astype(BF16),
                   preferred_element_type=F32)) / l
    o_ref[...] = _heads_to_lanes(lambda hh: o[hh * t_new:(hh + 1) * t_new, :], SWA_KV_HEADS,
                                 t_new).astype(o_ref.dtype)


def _swa_sample_attn(y, kb_t, vb_t, rel_bias, sinks, past, n_seq, t_new):
    rows = N_HEADS * t_new
    wb = kb_t.shape[2]
    i = jnp.arange(t_new)[:, None]
    r = jnp.arange(wb)[None, :]
    d_b = wb + i - r
    tb = _bias_rows(rel_bias, d_b, (d_b <= SWA_WINDOW) & (past - wb + r >= 0), t_new)
    ki = jnp.arange(LANES)[None, :]
    d_n = i - ki
    tn = _bias_rows(rel_bias, d_n, (d_n >= 0) & (ki < t_new), t_new)
    sink_rows = jnp.broadcast_to(jnp.repeat(sinks.astype(F32), t_new)[:, None], (rows, LANES))
    c = SWA_COLS
    return pl.pallas_call(
        functools.partial(_swa_sample_kernel, t_new=t_new),
        grid=(n_seq,),
        in_specs=[
            pl.BlockSpec((t_new, D_MODEL), lambda n: (n, 0)),
            pl.BlockSpec((t_new, LANES), lambda n: (n, c["k"])),
            pl.BlockSpec((t_new, LANES), lambda n: (n, c["v"])),
            pl.BlockSpec((None, LANES, wb), lambda n: (n, 0, 0)),
            pl.BlockSpec((None, LANES, wb), lambda n: (n, 0, 0)),
            pl.BlockSpec((rows, wb), lambda n: (0, 0)),
            pl.BlockSpec((rows, LANES), lambda n: (0, 0)),
            pl.BlockSpec((rows, LANES), lambda n: (0, 0)),
        ],
        out_specs=pl.BlockSpec((t_new, D_MODEL), lambda n: (n, 0)),
        out_shape=jax.ShapeDtypeStruct((n_seq * t_new, D_MODEL), BF16),
        compiler_params=_cparams("parallel"),
        name="swa_sample",
    )(y, y, y, kb_t, vb_t, tb, tn, sink_rows)


def _pad_cols(w, n):
    return jnp.pad(w, ((0, 0), (0, n - w.shape[1])))


def _compress_weight(w_phi):
    g_n = w_phi.shape[0]
    wp = jnp.transpose(w_phi, (1, 0, 2, 3))
    return jnp.einsum("jgde,gh->jgdhe", wp, jnp.eye(g_n, dtype=w_phi.dtype)).reshape(
        NSA_CMP_BLOCK * g_n * HEAD_DIM, g_n * HEAD_DIM)


def _compress_weight_rows(w_phi):
    bpp = PAGE_SIZE // NSA_CMP_BLOCK
    g_n = w_phi.shape[0]
    w = jnp.einsum("gjde,bc->gdbjce", w_phi, jnp.eye(bpp, dtype=w_phi.dtype))
    return w.reshape(g_n * HEAD_DIM, PAGE_SIZE, bpp * HEAD_DIM).astype(BF16)


def _expand_blocks(n_blocks_padded, n_keys):
    b = jnp.arange(n_blocks_padded)[:, None]
    k = jnp.arange(n_keys)[None, :]
    return (b == k // NSA_CMP_BLOCK).astype(BF16)


def _nsa_layer(xp, xs, mod_p, mod_s, slot, n_batch, seq, n_seq, t_new, pools, bufs, page_table, params,
               rel_bias, band_tabs):
    w_in, b_gate, w_phi_k, w_phi_v = params
    g_n = NSA_KV_HEADS
    kvw = g_n * HEAD_DIM
    w = _pad_cols(w_in, NSA_NTOT).astype(BF16)
    bias = jnp.zeros((NSA_NTOT,), F32)
    yp, ypb = _modmm(xp, mod_p[0], mod_p[1], w, bias, seq, second="b")
    ys, _ = _modmm(xs, mod_s[0], mod_s[1], w, bias, t_new, second="b")
    b_gate_row = jnp.pad(b_gate.astype(F32), (0, LANES - b_gate.shape[0])).reshape(1, LANES)
    wk = _compress_weight(w_phi_k)
    wv = _compress_weight(w_phi_v)
    col = lambda y, name: y[:, NSA_COLS[name] * LANES:(NSA_COLS[name] + 1) * LANES]

    nq = seq // Q_BLOCK
    nb = seq // NSA_CMP_BLOCK
    kc_p, vc_p = col(yp, "kc"), col(yp, "vc")
    flat_k = NSA_CMP_BLOCK * kvw
    kcs_p = _mm(kc_p.reshape(n_batch * nb, flat_k), wk, tm=min(256, n_batch * nb), tn=kvw, tk=2048)
    vcs_p = _mm(vc_p.reshape(n_batch * nb, flat_k), wv, tm=min(256, n_batch * nb), tn=kvw, tk=2048)
    t = jnp.arange(seq)[:, None]
    c_end = (jnp.arange(nb)[None, :] + 1) * NSA_CMP_BLOCK - 1
    d_c = t - c_end
    bc = _bias_heads(rel_bias, d_c, d_c >= 0).reshape(N_HEADS, nq, Q_BLOCK, nb).transpose(1, 0, 2, 3)
    bc = bc.reshape(nq, N_HEADS * Q_BLOCK, nb)
    ex = _expand_blocks(nb, seq).reshape(nb, nq, Q_BLOCK).transpose(1, 0, 2)
    op = _nsa_prompt_attn(yp, ypb, kcs_p, vcs_p, b_gate_row, band_tabs, bc, ex, n_batch, seq)

    n_pages = page_table.shape[1]
    past = n_pages * PAGE_SIZE
    pool_kc, pool_vc, pool_ks, pool_vs = pools
    n_pool = pool_kc.shape[1]
    nb_s = past // NSA_CMP_BLOCK + 1
    nbp = (nb_s + 15) // 16 * 16
    bpp = PAGE_SIZE // NSA_CMP_BLOCK

    def summaries(pool, new_rows, w_phi, wmat):
        s_all = _pool_compress(_pages_t(pool).reshape(-1, PAGE_SIZE), _compress_weight_rows(w_phi), slot,
                               n_pool, g_n)
        past_sum = s_all[page_table.reshape(-1)].reshape(n_seq, n_pages, g_n, bpp, HEAD_DIM)
        past_sum = jnp.transpose(past_sum, (0, 1, 3, 2, 4)).reshape(n_seq, n_pages * bpp, kvw)
        new_flat = new_rows.reshape(n_seq, t_new * kvw)
        last = _mm(new_flat, wmat[:t_new * kvw], tm=n_seq, tn=kvw, tk=t_new * kvw)
        pad = jnp.zeros((n_seq, nbp - nb_s, kvw), F32)
        return jnp.concatenate([past_sum, last[:, None, :], pad], axis=1)

    kcs_s = summaries(pool_kc, col(ys, "kc"), w_phi_k, wk)
    vcs_s = summaries(pool_vc, col(ys, "vc"), w_phi_v, wv)
    buf_kw, buf_vw = bufs
    wb = buf_kw.shape[1]
    tables = _nsa_sample_tables(rel_bias, past, t_new, wb, nb_s, nbp)
    ex_s = _expand_blocks(nbp, past)
    pt_flat = (page_table + slot * n_pool).reshape(-1).astype(jnp.int32)
    os_ = _nsa_sample_attn(ys, kcs_s, vcs_s, b_gate_row, tables, ex_s, _buf_t(buf_kw), _buf_t(buf_vw),
                           _pages_t(pool_ks), _pages_t(pool_vs), pt_flat, n_seq, t_new, n_pages, nb_s)

    shp_p = (n_batch, seq, NSA_KV_HEADS, HEAD_DIM)
    shp_s = (n_seq, t_new, NSA_KV_HEADS, HEAD_DIM)
    wbp = min(NSA_WINDOW, seq)
    st_p, st_s = [], []
    for name in ("kc", "vc", "ks", "vs"):
        st_p.append(col(yp, name).reshape(shp_p))
        st_s.append(col(ys, name).reshape(shp_s))
    for name, buf in (("kw", buf_kw), ("vw", buf_vw)):
        st_p.append(col(yp, name).reshape(shp_p)[:, seq - wbp:])
        st_s.append(jnp.concatenate([buf, col(ys, name).reshape(shp_s)], axis=1)[:, t_new:])
    return op, os_, st_p, st_s


def _sb_layer(xp, xs, mod_p, mod_s, slot, n_batch, seq, n_seq, t_new, pools, page_table, w_in):
    kvw = SB_KV_HEADS * HEAD_DIM
    w = w_in.astype(BF16)
    bias = jnp.zeros((SB_NTOT,), F32)
    yp, ypb = _modmm(xp, mod_p[0], mod_p[1], w, bias, seq, second="b")
    ys, _ = _modmm(xs, mod_s[0], mod_s[1], w, bias, t_new, second="b")
    op = _sb_prompt_attn(ypb, n_batch, seq)
    pool_k, pool_v = pools
    n_pool = pool_k.shape[1]
    n_pages = page_table.shape[1]
    pt_flat = (page_table + slot * n_pool).reshape(-1).astype(jnp.int32)
    os_ = _sb_sample_attn(ys, _pages_t(pool_k), _pages_t(pool_v), pt_flat, n_seq, t_new, n_pages)
    shp_p = (n_batch, seq, SB_KV_HEADS, HEAD_DIM)
    shp_s = (n_seq, t_new, SB_KV_HEADS, HEAD_DIM)
    st_p = [yp[:, D_MODEL:D_MODEL + kvw].reshape(shp_p), yp[:, D_MODEL + kvw:].reshape(shp_p)]
    st_s = [ys[:, D_MODEL:D_MODEL + kvw].reshape(shp_s), ys[:, D_MODEL + kvw:].reshape(shp_s)]
    return op, os_, st_p, st_s


def _swa_layer(xp, xs, mod_p, mod_s, n_batch, seq, n_seq, t_new, bufs, past, w_in, sinks, rel_bias):
    w = w_in.astype(BF16)
    bias = jnp.zeros((SWA_NTOT,), F32)
    yp, ypb = _modmm(xp, mod_p[0], mod_p[1], w, bias, seq, second="b")
    ys, _ = _modmm(xs, mod_s[0], mod_s[1], w, bias, t_new, second="b")
    rows = N_HEADS * Q_BLOCK
    tabs = _band_tables(rel_bias, SWA_WINDOW // Q_BLOCK + 1, SWA_WINDOW).reshape(-1, rows, Q_BLOCK)
    sink_rows = jnp.broadcast_to(jnp.repeat(sinks.astype(F32), Q_BLOCK)[:, None], (rows, LANES))
    op = _swa_prompt_attn(ypb, tabs, sink_rows, n_batch, seq)
    buf_k, buf_v = bufs
    os_ = _swa_sample_attn(ys, _buf_t(buf_k), _buf_t(buf_v), rel_bias, sinks, past, n_seq, t_new)
    shp_p = (n_batch, seq, SWA_KV_HEADS, HEAD_DIM)
    shp_s = (n_seq, t_new, SWA_KV_HEADS, HEAD_DIM)
    wbp = min(SWA_WINDOW, seq)
    col = lambda y, name: y[:, SWA_COLS[name] * LANES:(SWA_COLS[name] + 1) * LANES]
    st_p = [col(yp, n).reshape(shp_p)[:, seq - wbp:] for n in ("k", "v")]
    st_s = [jnp.concatenate([b, col(ys, n).reshape(shp_s)], axis=1)[:, t_new:]
            for n, b in (("k", buf_k), ("v", buf_v))]
    return op, os_, st_p, st_s


def _moe_layer(xp, xs, mod_p, mod_s, seq, t_new, layer, w_rg, b_rg, w_re, b_re, w_gate, w_up, w_down,
               ln_g, ln_b):
    tp = xp.shape[0]
    w_r = _pad_cols(jnp.concatenate([w_rg, w_re], axis=1), LANES).astype(BF16)
    b_r = jnp.pad(jnp.concatenate([b_rg, b_re]).astype(F32), (0, LANES - MOE_GROUPS - MOE_EXPERTS))
    lg_p, h_p = _modmm(xp, mod_p[0], mod_p[1], w_r, b_r, seq, second="h")
    lg_s, h_s = _modmm(xs, mod_s[0], mod_s[1], w_r, b_r, t_new, second="h")
    e_idx, e_w = _route(jnp.concatenate([lg_p, lg_s], axis=0))
    src, tile_e, tile_valid, dest = _moe_dispatch(e_idx, MOE_TILE)
    h_all = jnp.concatenate([h_p, h_s, jnp.zeros((8, D_MODEL), F32)], axis=0)
    out = _moe_ffn(h_all[src], tile_e + layer * MOE_EXPERTS, tile_valid, w_gate, w_up, w_down)
    f0, f1 = out[dest[:, 0]], out[dest[:, 1]]
    xp2 = _combine_res_ln(f0[:tp], f1[:tp], e_w[:tp], xp, mod_p[2], ln_g, ln_b, seq)
    xs2 = _combine_res_ln(f0[tp:], f1[tp:], e_w[tp:], xs, mod_s[2], ln_g, ln_b, t_new)
    return xp2, xs2


def kernel(x_prompt, x_sample, c_prompt, c_sample, cache_nsa_k_cmp, cache_nsa_v_cmp, cache_nsa_k_slc,
           cache_nsa_v_slc, cache_nsa_k_win, cache_nsa_v_win, cache_sb_k, cache_sb_v, cache_swa_k_win,
           cache_swa_v_win, page_table, rel_bias, w_ada, b_ada, ln1_g, ln1_b, ln2_g, ln2_b, w_in_nsa,
           b_gate_nsa, w_phi_k_nsa, w_phi_v_nsa, w_o_nsa, w_in_sb, w_o_sb, w_in_swa, sinks_swa, w_o_swa,
           w_route_group, b_route_group, w_route_expert, b_route_expert, w_exp_gate, w_exp_up, w_exp_down):
    n_batch, seq, _ = x_prompt.shape
    n_seq, t_new, _ = x_sample.shape
    past = page_table.shape[1] * PAGE_SIZE
    xp = x_prompt.reshape(n_batch * seq, D_MODEL)
    xs = x_sample.reshape(n_seq * t_new, D_MODEL)
    n_c = n_batch + n_seq
    n_c_pad = (n_c + 7) // 8 * 8
    c_all = jnp.concatenate([c_prompt, c_sample, jnp.zeros((n_c_pad - n_c, D_MODEL), F32)], axis=0)
    far_bias = rel_bias[REL_BUCKETS - 1].astype(F32)
    nsa_tabs = _band_tables(rel_bias, NSA_WINDOW // Q_BLOCK + 1, NSA_WINDOW, shift=far_bias).reshape(
        -1, N_HEADS * Q_BLOCK, Q_BLOCK)
    ts_tiles = (n_seq * t_new) // ROW_TILE
    w_gate_all = w_exp_gate.reshape(DEPTH * MOE_EXPERTS, D_MODEL, MOE_D_FF)
    w_up_all = w_exp_up.reshape(DEPTH * MOE_EXPERTS, D_MODEL, MOE_D_FF)
    w_down_all = w_exp_down.reshape(DEPTH * MOE_EXPERTS, MOE_D_FF, D_MODEL)

    st = {k: ([], []) for k in ("a", "b", "c")}
    for layer in range(DEPTH):
        kind, slot = layer % N_MIXERS, layer // N_MIXERS
        mods = _mm(c_all, w_ada[layer], b_ada[layer], act="silu", tm=n_c_pad, tn=1536, tk=D_MODEL)
        mp = [mods[:n_batch, k * D_MODEL:(k + 1) * D_MODEL].reshape(n_batch, 1, D_MODEL) for k in range(6)]
        ms = [jnp.repeat(mods[n_batch:n_c, k * D_MODEL:(k + 1) * D_MODEL], t_new, axis=0)
              .reshape(ts_tiles, ROW_TILE, D_MODEL) for k in range(6)]
        if kind == 0:
            op, os_, stp, sts = _nsa_layer(
                xp, xs, mp[0:2], ms[0:2], slot, n_batch, seq, n_seq, t_new,
                (cache_nsa_k_cmp, cache_nsa_v_cmp, cache_nsa_k_slc, cache_nsa_v_slc),
                (cache_nsa_k_win[slot], cache_nsa_v_win[slot]), page_table,
                (w_in_nsa[slot], b_gate_nsa[slot], w_phi_k_nsa[slot], w_phi_v_nsa[slot]), rel_bias, nsa_tabs)
            w_o, key = w_o_nsa[slot], "a"
        elif kind == 1:
            op, os_, stp, sts = _sb_layer(xp, xs, mp[0:2], ms[0:2], slot, n_batch, seq, n_seq, t_new,
                                          (cache_sb_k, cache_sb_v), page_table, w_in_sb[slot])
            w_o, key = w_o_sb[slot], "b"
        else:
            op, os_, stp, sts = _swa_layer(xp, xs, mp[0:2], ms[0:2], n_batch, seq, n_seq, t_new,
                                           (cache_swa_k_win[slot], cache_swa_v_win[slot]), past,
                                           w_in_swa[slot], sinks_swa[slot], rel_bias)
            w_o, key = w_o_swa[slot], "c"
        st[key][0].append(stp)
        st[key][1].append(sts)
        w_o = w_o.astype(BF16)
        xp = _proj_res_ln(op, xp, mp[2], w_o, ln1_g[layer], ln1_b[layer], seq)
        xs = _proj_res_ln(os_, xs, ms[2], w_o, ln1_g[layer], ln1_b[layer], t_new)
        xp, xs = _moe_layer(xp, xs, mp[3:6], ms[3:6], seq, t_new, layer, w_route_group[layer],
                            b_route_group[layer], w_route_expert[layer], b_route_expert[layer],
                            w_gate_all, w_up_all, w_down_all, ln2_g[layer], ln2_b[layer])

    outs = [xp.reshape(n_batch, seq, D_MODEL), xs.reshape(n_seq, t_new, D_MODEL)]
    for key, n_state in (("a", 6), ("b", 2), ("c", 2)):
        for i in range(n_state):
            outs.append(jnp.stack([s[i] for s in st[key][0]]))
            outs.append(jnp.stack([s[i] for s in st[key][1]]))
    return tuple(outs)
```
